```python
import math, functools
import jax, jax.numpy as jnp
from jax import lax
import numpy as np

D_MODEL = 1024
BATCH = 2
SEQ = 8192
DEPTH = 2
DEC_BATCH = 32
DEC_SEQ = 1
PAST_LEN = 8192
PAGE_SIZE = 128

N_HEADS = 8
HEAD_DIM = 64
D_ATTN = N_HEADS * HEAD_DIM
IDX_HEADS = 8
IDX_DIM = 64
TOPK_MAX = 256
Q_BLOCK = 128
D_RNN = D_MODEL
RNN_BLOCKS = 8
RNN_BW = D_RNN // RNN_BLOCKS
CONV_W = 4
RG_C = 8.0
D_FF = 4 * D_MODEL
D_PLE = 256
ALPHA = (2.0 * DEPTH) ** 0.25
BETA = (8.0 * DEPTH) ** -0.25
LN_EPS = 1e-5
IN_SPLITS = (D_ATTN, D_ATTN, D_ATTN, IDX_HEADS * IDX_DIM, IDX_DIM, IDX_HEADS, D_RNN, D_RNN, D_MODEL, D_MODEL)
N_IN = sum(IN_SPLITS)

kernel_name = "dsa_rglru_gated_hybrid_step"


def layer_norm(x, g, b):
    xf = x.astype(jnp.float32)
    mu = jnp.mean(xf, axis=-1, keepdims=True)
    var = jnp.mean(jnp.square(xf - mu), axis=-1, keepdims=True)
    return ((xf - mu) * lax.rsqrt(var + LN_EPS) * g.astype(jnp.float32) + b.astype(jnp.float32)).astype(x.dtype)


def split_cols(z):
    outs = []
    off = 0
    for n in IN_SPLITS:
        outs.append(z[..., off:off + n])
        off += n
    return outs


def index_scores(qi, wi, kidx):
    s = jnp.einsum('bthd,bsd->bths', qi, kidx).astype(jnp.float32) * (IDX_DIM ** -0.5)
    return jnp.einsum('bth,bths->bts', wi.astype(jnp.float32) * (IDX_HEADS ** -0.5), jax.nn.relu(s))


def select_and_attend(q, qi, wi, kidx_all, qpos, topk, gather_kv):
    L = kidx_all.shape[1]
    score = index_scores(qi, wi, kidx_all)
    admissible = jnp.arange(L)[None, None, :] <= qpos[None, :, None]
    score = jnp.where(admissible, score, -jnp.inf)
    _, idx = lax.top_k(score, topk)
    valid = idx <= qpos[None, :, None]
    k_sel, v_sel = gather_kv(idx)
    logits = jnp.einsum('bthd,btkhd->bthk', q, k_sel).astype(jnp.float32) * (HEAD_DIM ** -0.5)
    logits = jnp.where(valid[:, :, None, :], logits, -jnp.inf)
    probs = jax.nn.softmax(logits, axis=-1).astype(v_sel.dtype)
    return jnp.einsum('bthk,btkhd->bthd', probs, v_sel)


def rows_of(arr, idx):
    return jax.vmap(lambda a, i: a[i])(arr, idx)


def prompt_attention(q, k, v, qi, ki, wi):
    B, S = q.shape[0], q.shape[1]
    n_blocks = S // Q_BLOCK
    topk = min(TOPK_MAX, S // 4)

    def gather(idx):
        return rows_of(k, idx), rows_of(v, idx)

    def block(j):
        start = j * Q_BLOCK
        qb = lax.dynamic_slice_in_dim(q, start, Q_BLOCK, axis=1)
        qib = lax.dynamic_slice_in_dim(qi, start, Q_BLOCK, axis=1)
        wib = lax.dynamic_slice_in_dim(wi, start, Q_BLOCK, axis=1)
        qpos = start + jnp.arange(Q_BLOCK, dtype=jnp.int32)
        return select_and_attend(qb, qib, wib, ki, qpos, topk, gather)

    out = lax.map(block, jnp.arange(n_blocks, dtype=jnp.int32))
    return jnp.transpose(out, (1, 0, 2, 3, 4)).reshape(B, S, D_ATTN)


def sample_attention(q, k, v, qi, ki, wi, ck, cv, cki, page_table):
    Bd, T = q.shape[0], q.shape[1]
    n_pages = page_table.shape[1]
    past = n_pages * PAGE_SIZE
    topk = min(TOPK_MAX, (past + T) // 4)
    ki_past = cki[page_table].reshape(Bd, past, IDX_DIM)
    kidx_all = jnp.concatenate([ki_past.astype(ki.dtype), ki], axis=1)
    ck_flat = ck.reshape(-1, N_HEADS, HEAD_DIM)
    cv_flat = cv.reshape(-1, N_HEADS, HEAD_DIM)

    def gather(idx):
        ip = jnp.clip(idx, 0, past - 1)
        phys = jax.vmap(lambda pt, i: pt[i])(page_table, ip // PAGE_SIZE) * PAGE_SIZE + ip % PAGE_SIZE
        inew = jnp.clip(idx - past, 0, T - 1)
        is_past = (idx < past)[..., None, None]
        k_sel = jnp.where(is_past, ck_flat[phys].astype(k.dtype), rows_of(k, inew))
        v_sel = jnp.where(is_past, cv_flat[phys].astype(v.dtype), rows_of(v, inew))
        return k_sel, v_sel

    qpos = past + jnp.arange(T, dtype=jnp.int32)
    return select_and_attend(q, qi, wi, kidx_all, qpos, topk, gather).reshape(Bd, T, D_ATTN)


def rglru_branch(xr, gr, conv_buf, h0, conv_w, conv_b, wa, ba, wig, big, lam):
    B, T = xr.shape[0], xr.shape[1]
    xp = jnp.concatenate([conv_buf.astype(xr.dtype), xr], axis=1)
    xc = conv_b + sum(xp[:, j:j + T] * conv_w[j] for j in range(CONV_W))
    new_buf = xp[:, T:]
    xb = xc.reshape(B, T, RNN_BLOCKS, RNN_BW)
    r = jax.nn.sigmoid(jnp.einsum('btnc,ncd->btnd', xb, wa).reshape(B, T, D_RNN) + ba)
    ig = jax.nn.sigmoid(jnp.einsum('btnc,ncd->btnd', xb, wig).reshape(B, T, D_RNN) + big)
    log_a = -RG_C * r.astype(jnp.float32) * jax.nn.softplus(-lam.astype(jnp.float32))
    a = jnp.exp(log_a)
    bx = jnp.sqrt(-jnp.expm1(2.0 * log_a)) * (ig * xc).astype(jnp.float32)

    def step(h, ab):
        h = ab[0] * h + ab[1]
        return h, h

    h_last, hs = lax.scan(step, h0.astype(jnp.float32), (jnp.swapaxes(a, 0, 1), jnp.swapaxes(bx, 0, 1)))
    y = jnp.swapaxes(hs, 0, 1).astype(xr.dtype) * jax.nn.gelu(gr)
    return y, h_last, new_buf


def layer(x, pe, conv_buf, h0, attend, lw):
    (w_in, b_in, conv_w, conv_b, rg_wa, rg_ba, rg_wi, rg_bi, rg_lam, w_oa, w_ob, w_out,
     ln1_g, ln1_b, w_ff1, w_ff2, ln2_g, ln2_b, w_pe, w_pg, b_pg) = lw
    B, T = x.shape[0], x.shape[1]
    z = x @ w_in + b_in
    q, k, v, qi, ki, wi, xr, gr, ga, gb = split_cols(z)
    q = q.reshape(B, T, N_HEADS, HEAD_DIM)
    k = k.reshape(B, T, N_HEADS, HEAD_DIM)
    v = v.reshape(B, T, N_HEADS, HEAD_DIM)
    qi = qi.reshape(B, T, IDX_HEADS, IDX_DIM)
    o_attn = attend(q, k, v, qi, ki, wi)
    y_rnn, h_last, new_buf = rglru_branch(xr, gr, conv_buf, h0, conv_w, conv_b, rg_wa, rg_ba, rg_wi, rg_bi, rg_lam)
    merged = jax.nn.sigmoid(ga) * (o_attn @ w_oa) + jax.nn.sigmoid(gb) * (y_rnn @ w_ob)
    x = layer_norm(ALPHA * x + merged @ w_out, ln1_g, ln1_b)
    ff = jnp.square(jax.nn.relu(x @ w_ff1)) @ w_ff2
    x = layer_norm(ALPHA * x + ff, ln2_g, ln2_b)
    x = x + jax.nn.sigmoid(x @ w_pg + b_pg) * (pe @ w_pe)
    return x, (k, v, ki, h_last, new_buf)


def setup_inputs(seed: int = 0) -> dict:
    key = jax.random.key(seed)
    ks = jax.random.split(key, 40)
    f32 = jnp.float32
    nrm = lambda k, shape, s: jax.random.normal(k, shape, f32) * s
    n_pages = PAST_LEN // PAGE_SIZE
    n_used = DEC_BATCH * n_pages
    n_pool = n_used + max(1, n_used // 4)
    page_table = jax.random.permutation(ks[0], n_pool)[:n_used].reshape(DEC_BATCH, n_pages).astype(jnp.int32)
    a0 = jax.random.uniform(ks[1], (DEPTH, D_RNN), f32, 0.9, 0.999)
    s0 = a0 ** (1.0 / RG_C)
    rg_lam = jnp.log(s0) - jnp.log1p(-s0)
    return {
        'x_prompt': nrm(ks[2], (BATCH, SEQ, D_MODEL), 1.0),
        'x_sample': nrm(ks[3], (DEC_BATCH, DEC_SEQ, D_MODEL), 1.0),
        'p_prompt': nrm(ks[4], (DEPTH, BATCH, SEQ, D_PLE), 1.0),
        'p_sample': nrm(ks[5], (DEPTH, DEC_BATCH, DEC_SEQ, D_PLE), 1.0),
        'cache_k': nrm(ks[6], (DEPTH, n_pool, PAGE_SIZE, N_HEADS, HEAD_DIM), 1.0),
        'cache_v': nrm(ks[7], (DEPTH, n_pool, PAGE_SIZE, N_HEADS, HEAD_DIM), 1.0),
        'cache_kidx': nrm(ks[8], (DEPTH, n_pool, PAGE_SIZE, IDX_DIM), 1.0),
        'state_h': nrm(ks[9], (DEPTH, DEC_BATCH, D_RNN), 0.5),
        'state_conv': nrm(ks[10], (DEPTH, DEC_BATCH, CONV_W - 1, D_RNN), 1.0),
        'page_table': page_table,
        'w_in': nrm(ks[11], (DEPTH, D_MODEL, N_IN), D_MODEL ** -0.5),
        'b_in': nrm(ks[12], (DEPTH, N_IN), 0.01),
        'conv_w': nrm(ks[13], (DEPTH, CONV_W, D_RNN), CONV_W ** -0.5),
        'conv_b': nrm(ks[14], (DEPTH, D_RNN), 0.01),
        'rg_wa': nrm(ks[15], (DEPTH, RNN_BLOCKS, RNN_BW, RNN_BW), RNN_BW ** -0.5),
        'rg_ba': nrm(ks[16], (DEPTH, D_RNN), 0.01),
        'rg_wi': nrm(ks[17], (DEPTH, RNN_BLOCKS, RNN_BW, RNN_BW), RNN_BW ** -0.5),
        'rg_bi': nrm(ks[18], (DEPTH, D_RNN), 0.01),
        'rg_lam': rg_lam,
        'w_oa': nrm(ks[19], (DEPTH, D_ATTN, D_MODEL), BETA * D_ATTN ** -0.5),
        'w_ob': nrm(ks[20], (DEPTH, D_RNN, D_MODEL), BETA * D_RNN ** -0.5),
        'w_out': nrm(ks[21], (DEPTH, D_MODEL, D_MODEL), BETA * D_MODEL ** -0.5),
        'ln1_g': 1.0 + nrm(ks[22], (DEPTH, D_MODEL), 0.01),
        'ln1_b': nrm(ks[23], (DEPTH, D_MODEL), 0.01),
        'w_ff1': nrm(ks[24], (DEPTH, D_MODEL, D_FF), D_MODEL ** -0.5),
        'w_ff2': nrm(ks[25], (DEPTH, D_FF, D_MODEL), BETA * D_FF ** -0.5),
        'ln2_g': 1.0 + nrm(ks[26], (DEPTH, D_MODEL), 0.01),
        'ln2_b': nrm(ks[27], (DEPTH, D_MODEL), 0.01),
        'w_pe': nrm(ks[28], (DEPTH, D_PLE, D_MODEL), BETA * D_PLE ** -0.5),
        'w_pg': nrm(ks[29], (DEPTH, D_MODEL, D_MODEL), D_MODEL ** -0.5),
        'b_pg': nrm(ks[30], (DEPTH, D_MODEL), 0.01),
    }


def reference(x_prompt, x_sample, p_prompt, p_sample, cache_k, cache_v, cache_kidx, state_h, state_conv, page_table,
              w_in, b_in, conv_w, conv_b, rg_wa, rg_ba, rg_wi, rg_bi, rg_lam, w_oa, w_ob, w_out,
              ln1_g, ln1_b, w_ff1, w_ff2, ln2_g, ln2_b, w_pe, w_pg, b_pg):
    xp, xs = x_prompt, x_sample
    bp = x_prompt.shape[0]
    zero_buf = jnp.zeros((bp, CONV_W - 1, D_RNN), x_prompt.dtype)
    zero_h = jnp.zeros((bp, D_RNN), jnp.float32)
    kp, vp, kip, hp, cp = [], [], [], [], []
    ksm, vsm, kism, hsm, csm = [], [], [], [], []
    for l in range(DEPTH):
        lw = (w_in[l], b_in[l], conv_w[l], conv_b[l], rg_wa[l], rg_ba[l], rg_wi[l], rg_bi[l], rg_lam[l],
              w_oa[l], w_ob[l], w_out[l], ln1_g[l], ln1_b[l], w_ff1[l], w_ff2[l], ln2_g[l], ln2_b[l],
              w_pe[l], w_pg[l], b_pg[l])
        xp, (k1, v1, ki1, h1, c1) = layer(xp, p_prompt[l], zero_buf, zero_h, prompt_attention, lw)
        attend_s = functools.partial(sample_attention, ck=cache_k[l], cv=cache_v[l], cki=cache_kidx[l],
                                     page_table=page_table)
        xs, (k2, v2, ki2, h2, c2) = layer(xs, p_sample[l], state_conv[l], state_h[l], attend_s, lw)
        kp.append(k1); vp.append(v1); kip.append(ki1); hp.append(h1); cp.append(c1)
        ksm.append(k2); vsm.append(v2); kism.append(ki2); hsm.append(h2); csm.append(c2)
    return (xp, xs,
            jnp.stack(kp), jnp.stack(vp), jnp.stack(kip), jnp.stack(hp), jnp.stack(cp),
            jnp.stack(ksm), jnp.stack(vsm), jnp.stack(kism), jnp.stack(hsm), jnp.stack(csm))
```

```python
import functools
import math

import numpy as np
import jax
import jax.numpy as jnp
from jax import lax
from jax.experimental import pallas as pl
from jax.experimental.pallas import tpu as pltpu

F32 = jnp.float32
BF16 = jnp.bfloat16
I32 = jnp.int32

N_HEADS = 8
HEAD_DIM = 64
D_ATTN = N_HEADS * HEAD_DIM
IDX_HEADS = 8
IDX_DIM = 64
TOPK_MAX = 256
PAGE_SIZE = 128
RNN_BLOCKS = 8
CONV_W = 4
RG_C = 8.0
LN_EPS = 1e-5

LANES = 128
VMEM_LIMIT = 56 * 1024 * 1024
NEG_BIAS = -1e30
F32_LOWEST = float(np.finfo(np.float32).min)

_SEG = {}
_off = 0
for _name, _n in (("q", D_ATTN), ("k", D_ATTN), ("v", D_ATTN), ("qi", IDX_HEADS * IDX_DIM),
                  ("ki", LANES), ("wi", LANES)):
    _SEG[_name] = (_off, _off + _n)
    _off += _n
_N_HEAD_COLS = _off


def _cparams(sem):
    return pltpu.CompilerParams(dimension_semantics=sem, vmem_limit_bytes=VMEM_LIMIT)


def _resident(shape):
    nd = len(shape)
    return pl.BlockSpec(shape, lambda *_: (0,) * nd, pipeline_mode=pl.Buffered(1))


def _sigmoid(x):
    return 1.0 / (1.0 + jnp.exp(-x))


def _gelu_tanh(x):
    c = math.sqrt(2.0 / math.pi)
    return 0.5 * x * (1.0 + jnp.tanh(c * (x + 0.044715 * (x * x * x))))


def _expm1(x):
    u = jnp.exp(x)
    um1 = u - 1.0
    return jnp.where(u == 1.0, x, jnp.where(x < -0.5, um1, um1 * x / jnp.log(u)))


def _layer_norm(x, g, b):
    mu = jnp.mean(x, axis=-1, keepdims=True)
    xc = x - mu
    var = jnp.mean(xc * xc, axis=-1, keepdims=True)
    return xc * lax.rsqrt(var + LN_EPS) * g + b


def _split_hi_lo(x):
    hi = x.astype(BF16)
    lo = (x - hi.astype(F32)).astype(BF16)
    return hi, lo


def _swap_halves(t):
    return jnp.concatenate([t[:, HEAD_DIM:], t[:, :HEAD_DIM]], axis=1)


def _inproj_kernel(x_ref, w_ref, b_ref, wr_ref, br_ref, *out_refs, prompt, d_rnn):
    if prompt:
        (q_ref, kxt_ref, vx_ref, k_ref, v_ref, qp_ref, kt_ref, ki_ref, wi_ref,
         xr_ref, gr_ref, ga_ref, gb_ref) = out_refs
    else:
        (q_ref, k_ref, v_ref, qp_ref, ki_ref, wi_ref, ss_ref,
         xr_ref, gr_ref, ga_ref, gb_ref) = out_refs
    xb = x_ref[...].astype(BF16)
    tm = xb.shape[0]

    def seg(name):
        a, b = _SEG[name]
        return jnp.dot(xb, w_ref[:, a:b], preferred_element_type=F32) + b_ref[:, a:b]

    lane = lax.broadcasted_iota(I32, (tm, LANES), 1)
    low = lane < HEAD_DIM

    zq = seg("q") * (HEAD_DIM ** -0.5)
    if prompt:
        zqb = zq.astype(BF16)
        for p in range(N_HEADS // 2):
            q_ref[p] = zqb[:, p * LANES:(p + 1) * LANES]
    else:
        q_ref[...] = zq

    zk = seg("k")
    k_ref[...] = zk
    zv = seg("v")
    v_ref[...] = zv
    if prompt:
        sub = lax.broadcasted_iota(I32, (LANES, tm), 0)
        zvb = zv.astype(BF16)
        zero_b = jnp.zeros((tm, LANES), BF16)
        for p in range(N_HEADS // 2):
            kt = zk[:, p * LANES:(p + 1) * LANES].T
            vt = zvb[:, p * LANES:(p + 1) * LANES]
            for e in range(2):
                h = 2 * p + e
                keep_rows = (sub < HEAD_DIM) if e == 0 else (sub >= HEAD_DIM)
                kxt_ref[h] = jnp.where(keep_rows, kt, 0.0).astype(BF16)
                keep_lanes = low if e == 0 else jnp.logical_not(low)
                vx_ref[h] = jnp.where(keep_lanes, vt, zero_b)

    zqi = seg("qi") * (IDX_DIM ** -0.5)
    zki = seg("ki")
    ki_ref[...] = zki[:, :IDX_DIM]
    zwi = seg("wi") * (IDX_HEADS ** -0.5)
    wi_ref[...] = zwi[:, :IDX_HEADS]
    for p in range(IDX_HEADS // 2):
        t = zqi[:, p * LANES:(p + 1) * LANES]
        sw = _swap_halves(t)
        for e in range(2):
            h = 2 * p + e
            dup = jnp.where(low, t, sw) if e == 0 else jnp.where(low, sw, t)
            hi, lo = _split_hi_lo(dup)
            qp_ref[h, :, 0:LANES] = hi
            qp_ref[h, :, LANES:2 * LANES] = lo
    if prompt:
        kit = zki.T[:IDX_DIM]
        hi, lo = _split_hi_lo(kit)
        kt_ref[0 * IDX_DIM:1 * IDX_DIM] = hi
        kt_ref[1 * IDX_DIM:2 * IDX_DIM] = lo
        kt_ref[2 * IDX_DIM:3 * IDX_DIM] = hi
        kt_ref[3 * IDX_DIM:4 * IDX_DIM] = lo
    else:
        acc = jnp.zeros((tm, 1), F32)
        for p in range(IDX_HEADS // 2):
            t = zqi[:, p * LANES:(p + 1) * LANES]
            kk = jnp.where(low, zki, _swap_halves(zki))
            prod = t * kk
            for e in range(2):
                h = 2 * p + e
                keep = low if e == 0 else jnp.logical_not(low)
                s_h = jnp.sum(jnp.where(keep, prod, 0.0), axis=1, keepdims=True)
                acc = acc + zwi[:, h:h + 1] * jnp.maximum(s_h, 0.0)
        ss_ref[...] = acc

    for i, ref in enumerate((xr_ref, gr_ref, ga_ref, gb_ref)):
        a, b = i * d_rnn, (i + 1) * d_rnn
        ref[...] = jnp.dot(xb, wr_ref[:, a:b], preferred_element_type=F32) + br_ref[:, a:b]


def _inproj(x2d, w_head, b_head, w_rest, b_rest, *, prompt, tm):
    m, d = x2d.shape
    d_rnn = w_rest.shape[1] // 4
    sds = jax.ShapeDtypeStruct
    row = lambda n: pl.BlockSpec((tm, n), lambda i: (i, 0))
    wide = [sds((m, d_rnn), F32)] * 4
    wide_specs = [row(d_rnn)] * 4
    qp_shape = sds((IDX_HEADS, m, 2 * LANES), BF16)
    qp_spec = pl.BlockSpec((IDX_HEADS, tm, 2 * LANES), lambda i: (0, i, 0))
    if prompt:
        out_shape = [sds((N_HEADS // 2, m, LANES), BF16), sds((N_HEADS, LANES, m), BF16),
                     sds((N_HEADS, m, LANES), BF16), sds((m, D_ATTN), F32), sds((m, D_ATTN), F32),
                     qp_shape, sds((4 * IDX_DIM, m), BF16), sds((m, IDX_DIM), F32),
                     sds((m, IDX_HEADS), F32)] + wide
        out_specs = [pl.BlockSpec((N_HEADS // 2, tm, LANES), lambda i: (0, i, 0)),
                     pl.BlockSpec((N_HEADS, LANES, tm), lambda i: (0, 0, i)),
                     pl.BlockSpec((N_HEADS, tm, LANES), lambda i: (0, i, 0)),
                     row(D_ATTN), row(D_ATTN), qp_spec,
                     pl.BlockSpec((4 * IDX_DIM, tm), lambda i: (0, i)),
                     row(IDX_DIM), row(IDX_HEADS)] + wide_specs
    else:
        out_shape = [sds((m, D_ATTN), F32), sds((m, D_ATTN), F32), sds((m, D_ATTN), F32),
                     qp_shape, sds((m, IDX_DIM), F32), sds((m, IDX_HEADS), F32), sds((m, 1), F32)] + wide
        out_specs = [row(D_ATTN), row(D_ATTN), row(D_ATTN), qp_spec,
                     row(IDX_DIM), row(IDX_HEADS), row(1)] + wide_specs
    return pl.pallas_call(
        functools.partial(_inproj_kernel, prompt=prompt, d_rnn=d_rnn),
        out_shape=out_shape,
        grid=(m // tm,),
        in_specs=[row(d), _resident(w_head.shape), _resident(b_head.shape),
                  _resident(w_rest.shape), _resident(b_rest.shape)],
        out_specs=out_specs,
        compiler_params=_cparams(("parallel",)),
        name="inproj_prompt" if prompt else "inproj_sample",
    )(x2d, w_head, b_head, w_rest, b_rest)


def _ukey_to_f32(u):
    sk = u ^ jnp.int32(-2 ** 31)
    bits = jnp.where(sk >= 0, sk, sk ^ jnp.int32(2 ** 31 - 1))
    return lax.bitcast_convert_type(bits, F32)


def _select_to_bias(s_ref, o_ref, nch, n_adm, *, rows, chunk, k, ncols):
    kf = float(k)
    tiles = chunk // LANES
    lane = lax.broadcasted_iota(I32, (rows, LANES), 1)

    def count(pred):
        def body(c, acc):
            off = pl.multiple_of(c * chunk, chunk)
            for u in range(tiles):
                t = s_ref[:, pl.ds(off + u * LANES, LANES)]
                acc = acc + pred(t, off + u * LANES)
            return acc
        acc = lax.fori_loop(0, nch, body, jnp.zeros((rows, LANES), F32))
        return jnp.sum(acc, axis=1, keepdims=True)

    def wide(v):
        return jnp.broadcast_to(v, (rows, LANES))

    def bit_body(i, cur):
        trial = cur | lax.shift_left(jnp.int32(1), 31 - i)
        thr_w = wide(_ukey_to_f32(trial))
        cnt = count(lambda t, o: jnp.where(t >= thr_w, 1.0, 0.0))
        return jnp.where(cnt >= kf, trial, cur)

    cur = lax.fori_loop(0, 32, bit_body, jnp.zeros((rows, 1), I32))
    search = n_adm > k
    thr = jnp.where(search, _ukey_to_f32(cur), F32_LOWEST)
    thr_w = wide(thr)
    cnt_ge = count(lambda t, o: jnp.where(t >= thr_w, 1.0, 0.0))
    tie = jnp.logical_and(search, cnt_ge > kf)
    n_tie = jnp.sum(jnp.where(tie, 1.0, 0.0))

    def write(bias_fn):
        def body(c, carry):
            off = pl.multiple_of(c * chunk, chunk)
            for u in range(tiles):
                o = off + u * LANES
                t = s_ref[:, pl.ds(o, LANES)]
                o_ref[:, pl.ds(o, LANES)] = bias_fn(t, o).astype(o_ref.dtype)
            return carry
        lax.fori_loop(0, nch, body, 0)

    @pl.when(n_tie == 0.0)
    def _():
        write(lambda t, o: jnp.where(t >= thr_w, 0.0, NEG_BIAS))

    @pl.when(n_tie > 0.0)
    def _():
        cnt_gt = count(lambda t, o: jnp.where(t > thr_w, 1.0, 0.0))
        need = jnp.where(search, kf - cnt_gt, float(2 * ncols))
        nbits = int(ncols).bit_length()

        def jbody(i, curj):
            trial = curj | lax.shift_left(jnp.int32(1), nbits - 1 - i)
            trial_w = wide(trial)
            g = count(lambda t, o: jnp.where(t == thr_w, jnp.where(lane + o < trial_w, 1.0, 0.0), 0.0))
            return jnp.where(g < need, trial, curj)

        jmax_w = wide(lax.fori_loop(0, nbits, jbody, jnp.zeros((rows, 1), I32)))
        write(lambda t, o: jnp.where(t == thr_w, jnp.where(lane + o <= jmax_w, 0.0, NEG_BIAS),
                                     jnp.where(t > thr_w, 0.0, NEG_BIAS)))


def _index_topk_kernel(q_ref, w_ref, kt_ref, o_ref, s_ref, *, seq, rows, chunk, k):
    j = pl.program_id(1)
    nch = (j * rows + rows + chunk - 1) // chunk
    qs = q_ref[...].reshape(IDX_HEADS * rows, 2 * LANES)
    w = w_ref[...]
    pos = j * rows + lax.broadcasted_iota(I32, (rows, chunk), 0)
    lane = lax.broadcasted_iota(I32, (rows, chunk), 1)

    def score_body(c, carry):
        off = pl.multiple_of(c * chunk, chunk)
        s8 = jnp.dot(qs, kt_ref[:, pl.ds(off, chunk)], preferred_element_type=F32)
        acc = w[:, 0:1] * jnp.maximum(s8[0:rows], 0.0)
        for h in range(1, IDX_HEADS):
            acc = acc + w[:, h:h + 1] * jnp.maximum(s8[h * rows:(h + 1) * rows], 0.0)
        s_ref[:, pl.ds(off, chunk)] = jnp.where(lane + off <= pos, acc, -jnp.inf)
        return carry

    lax.fori_loop(0, nch, score_body, 0)
    n_adm = j * rows + lax.broadcasted_iota(I32, (rows, 1), 0) + 1
    _select_to_bias(s_ref, o_ref, nch, n_adm, rows=rows, chunk=chunk, k=k, ncols=seq)

    def fill(c, carry):
        off = pl.multiple_of(c * chunk, chunk)
        o_ref[:, pl.ds(off, chunk)] = jnp.full((rows, chunk), NEG_BIAS, o_ref.dtype)
        return carry

    lax.fori_loop(nch, seq // chunk, fill, 0)


def _index_topk(qp, wi, kt, *, batch, seq, k):
    rows = LANES
    chunk = min(512, seq)
    nqb = seq // rows
    m = batch * seq
    return pl.pallas_call(
        functools.partial(_index_topk_kernel, seq=seq, rows=rows, chunk=chunk, k=k),
        out_shape=jax.ShapeDtypeStruct((m, seq), BF16),
        grid=(batch, nqb),
        in_specs=[pl.BlockSpec((IDX_HEADS, rows, 2 * LANES), lambda b, j: (0, b * nqb + j, 0)),
                  pl.BlockSpec((rows, IDX_HEADS), lambda b, j: (b * nqb + j, 0)),
                  pl.BlockSpec((4 * IDX_DIM, seq), lambda b, j: (0, b))],
        out_specs=pl.BlockSpec((rows, seq), lambda b, j: (b * nqb + j, 0)),
        scratch_shapes=[pltpu.VMEM((rows, seq), F32)],
        compiler_params=_cparams(("parallel", "arbitrary")),
        name="index_topk_prompt",
    )(qp, wi, kt)


def _attn_kernel(qt_ref, kt_ref, q_ref, kxt_ref, vx_ref, b_ref, o_ref, m_ref, l_ref, acc_ref, *, tq, tk):
    p = pl.program_id(1)
    kt = kt_ref[p]
    last = (qt_ref[p] * tq + tq - 1) // tk

    @pl.when(kt == 0)
    def _():
        m_ref[...] = jnp.full(m_ref.shape, NEG_BIAS, F32)
        l_ref[...] = jnp.zeros(l_ref.shape, F32)
        acc_ref[...] = jnp.zeros(acc_ref.shape, F32)

    bias = b_ref[...].astype(F32)
    low = lax.broadcasted_iota(I32, (tq, LANES), 1) < HEAD_DIM
    for pr in range(N_HEADS // 2):
        qp = q_ref[pr]
        alphas, pvs = [], []
        for e in range(2):
            h = 2 * pr + e
            s = jnp.dot(qp, kxt_ref[h], preferred_element_type=F32) + bias
            m_prev = m_ref[h]
            m_new = jnp.maximum(m_prev, jnp.max(s, axis=1, keepdims=True))
            alpha = jnp.exp(m_prev - m_new)
            pm = jnp.exp(s - m_new)
            l_ref[h] = alpha * l_ref[h] + jnp.sum(pm, axis=1, keepdims=True)
            m_ref[h] = m_new
            pvs.append(jnp.dot(pm.astype(BF16), vx_ref[h], preferred_element_type=F32))
            alphas.append(alpha)
        a = jnp.where(low, alphas[0], alphas[1])
        acc_ref[pr] = acc_ref[pr] * a + (pvs[0] + pvs[1])

    @pl.when(kt == last)
    def _():
        for pr in range(N_HEADS // 2):
            l = jnp.where(low, l_ref[2 * pr], l_ref[2 * pr + 1])
            o_ref[:, pr * LANES:(pr + 1) * LANES] = (acc_ref[pr] / l).astype(o_ref.dtype)


def _attention(q4, kxt, vx, bias, *, batch, seq):
    tq = min(256, seq)
    tk = min(512, seq)
    nq, nk = seq // tq, seq // tk
    qt, kt = [], []
    for i in range(nq):
        for c in range((i * tq + tq - 1) // tk + 1):
            qt.append(i)
            kt.append(c)
    npairs = len(qt)
    m = batch * seq
    return pl.pallas_call(
        functools.partial(_attn_kernel, tq=tq, tk=tk),
        out_shape=jax.ShapeDtypeStruct((m, D_ATTN), BF16),
        grid_spec=pltpu.PrefetchScalarGridSpec(
            num_scalar_prefetch=2,
            grid=(batch, npairs),
            in_specs=[
                pl.BlockSpec((N_HEADS // 2, tq, LANES), lambda b, p, qt, kt: (0, b * nq + qt[p], 0)),
                pl.BlockSpec((N_HEADS, LANES, tk), lambda b, p, qt, kt: (0, 0, b * nk + kt[p])),
                pl.BlockSpec((N_HEADS, tk, LANES), lambda b, p, qt, kt: (0, b * nk + kt[p], 0)),
                pl.BlockSpec((tq, tk), lambda b, p, qt, kt: (b * nq + qt[p], kt[p])),
            ],
            out_specs=pl.BlockSpec((tq, D_ATTN), lambda b, p, qt, kt: (b * nq + qt[p], 0)),
            scratch_shapes=[pltpu.VMEM((N_HEADS, tq, 1), F32), pltpu.VMEM((N_HEADS, tq, 1), F32),
                            pltpu.VMEM((N_HEADS // 2, tq, LANES), F32)],
        ),
        compiler_params=_cparams(("parallel", "arbitrary")),
        name="sparse_attention_prompt",
    )(jnp.asarray(qt, I32), jnp.asarray(kt, I32), q4, kxt, vx, bias)


def _rglru_gates(xc, wa_ref, ba_ref, wg_ref, bg_ref, lam_ref, store):
    xcb = xc.astype(BF16)
    lam = lam_ref[...]
    nlam = -lam
    softplus = jnp.maximum(nlam, 0.0) + jnp.log1p(jnp.exp(-jnp.abs(nlam)))
    bw = xc.shape[1] // RNN_BLOCKS
    for n in range(RNN_BLOCKS):
        sl = slice(n * bw, (n + 1) * bw)
        r = _sigmoid(jnp.dot(xcb[:, sl], wa_ref[n], preferred_element_type=F32) + ba_ref[:, sl])
        ig = _sigmoid(jnp.dot(xcb[:, sl], wg_ref[n], preferred_element_type=F32) + bg_ref[:, sl])
        log_a = (-RG_C) * r * softplus[:, sl]
        a = jnp.exp(log_a)
        bx = jnp.sqrt(-_expm1(2.0 * log_a)) * (ig * xc[:, sl])
        store(sl, a, bx)


def _rglru_kernel(xr_ref, gr_ref, cw_ref, cb_ref, wa_ref, ba_ref, wg_ref, bg_ref, lam_ref,
                  y_ref, hl_ref, xbuf, a_s, b_s, h_s, *, tc):
    c = pl.program_id(1)
    pad = 8

    @pl.when(c == 0)
    def _():
        xbuf[0:pad] = jnp.zeros((pad, xbuf.shape[1]), F32)
        h_s[...] = jnp.zeros(h_s.shape, F32)

    x = xr_ref[...]
    xbuf[pad:pad + tc] = x
    cw = cw_ref[...]
    xc = cb_ref[...] + cw[CONV_W - 1:CONV_W] * x
    for jj in range(1, CONV_W):
        xc = xc + cw[CONV_W - 1 - jj:CONV_W - jj] * xbuf[pad - jj:pad - jj + tc]
    xbuf[0:pad] = x[tc - pad:tc]

    def store(sl, a, bx):
        a_s[:, sl] = a
        b_s[:, sl] = bx

    _rglru_gates(xc, wa_ref, ba_ref, wg_ref, bg_ref, lam_ref, store)

    def step(t, h):
        h = a_s[pl.ds(t, 1), :] * h + b_s[pl.ds(t, 1), :]
        b_s[pl.ds(t, 1), :] = h
        return h

    h = lax.fori_loop(0, tc, step, h_s[...], unroll=8)
    h_s[...] = h
    hl_ref[0] = h
    y_ref[...] = (b_s[...] * _gelu_tanh(gr_ref[...])).astype(y_ref.dtype)


def _rglru(xr, gr, cw, cb, wa, ba, wg, bg, lam, *, batch, seq):
    tc = min(512, seq)
    nc = seq // tc
    m, c = xr.shape
    row = pl.BlockSpec((tc, c), lambda b, i: (b * nc + i, 0))
    vec = pl.BlockSpec((1, c), lambda b, i: (0, 0))
    blk = pl.BlockSpec(wa.shape, lambda b, i: (0, 0, 0))
    return pl.pallas_call(
        functools.partial(_rglru_kernel, tc=tc),
        out_shape=[jax.ShapeDtypeStruct((m, c), BF16), jax.ShapeDtypeStruct((batch, 1, c), F32)],
        grid=(batch, nc),
        in_specs=[row, row, pl.BlockSpec((CONV_W, c), lambda b, i: (0, 0)), vec, blk, vec, blk, vec, vec],
        out_specs=[row, pl.BlockSpec((1, 1, c), lambda b, i: (b, 0, 0))],
        scratch_shapes=[pltpu.VMEM((tc + 8, c), F32), pltpu.VMEM((tc, c), F32), pltpu.VMEM((tc, c), F32),
                        pltpu.VMEM((1, c), F32)],
        compiler_params=_cparams(("parallel", "arbitrary")),
        name="rglru_prompt",
    )(xr, gr, cw, cb, wa, ba, wg, bg, lam)


def _rglru_step_kernel(xr_ref, gr_ref, sc_ref, h0_ref, cw_ref, cb_ref, wa_ref, ba_ref, wg_ref, bg_ref, lam_ref,
                       y_ref, h_ref):
    x = xr_ref[...]
    cw = cw_ref[...]
    xc = cb_ref[...] + cw[CONV_W - 1:CONV_W] * x
    for jj in range(CONV_W - 1):
        xc = xc + cw[jj:jj + 1] * sc_ref[jj]
    h0 = h0_ref[...]

    def store(sl, a, bx):
        h_ref[:, sl] = a * h0[:, sl] + bx

    _rglru_gates(xc, wa_ref, ba_ref, wg_ref, bg_ref, lam_ref, store)
    y_ref[...] = (h_ref[...] * _gelu_tanh(gr_ref[...])).astype(y_ref.dtype)


def _rglru_step(xr, gr, sc, h0, cw, cb, wa, ba, wg, bg, lam):
    m, c = xr.shape
    return pl.pallas_call(
        _rglru_step_kernel,
        out_shape=[jax.ShapeDtypeStruct((m, c), BF16), jax.ShapeDtypeStruct((m, c), F32)],
        compiler_params=pltpu.CompilerParams(vmem_limit_bytes=VMEM_LIMIT),
        name="rglru_sample",
    )(xr, gr, sc, h0, cw, cb, wa, ba, wg, bg, lam)


def _post_kernel(x_ref, o_ref, y_ref, ga_ref, gb_ref, pe_ref,
                 woa_ref, wob_ref, wout_ref, g1_ref, b1_ref, wf1_ref, wf2_ref, g2_ref, b2_ref,
                 wpe_ref, wpg_ref, bpg_ref, out_ref, *, alpha, ff_chunk):
    dot = functools.partial(jnp.dot, preferred_element_type=F32)
    merged = (_sigmoid(ga_ref[...]) * dot(o_ref[...], woa_ref[...])
              + _sigmoid(gb_ref[...]) * dot(y_ref[...], wob_ref[...]))
    x1 = _layer_norm(alpha * x_ref[...] + dot(merged.astype(BF16), wout_ref[...]), g1_ref[...], b1_ref[...])
    x1b = x1.astype(BF16)
    d_ff = wf1_ref.shape[1]
    ff = None
    for c0 in range(0, d_ff, ff_chunk):
        hcol = jnp.maximum(dot(x1b, wf1_ref[:, c0:c0 + ff_chunk]), 0.0)
        part = dot((hcol * hcol).astype(BF16), wf2_ref[c0:c0 + ff_chunk, :])
        ff = part if ff is None else ff + part
    x2 = _layer_norm(alpha * x1 + ff, g2_ref[...], b2_ref[...])
    gate = _sigmoid(dot(x2.astype(BF16), wpg_ref[...]) + bpg_ref[...])
    out_ref[...] = x2 + gate * dot(pe_ref[...].astype(BF16), wpe_ref[...])


def _post(x, o, y, ga, gb, pe, lw, *, alpha, tm, name):
    m, d = x.shape
    row = lambda n: pl.BlockSpec((tm, n), lambda i: (i, 0))
    weights = (lw["w_oa"], lw["w_ob"], lw["w_out"], lw["ln1_g"], lw["ln1_b"], lw["w_ff1"], lw["w_ff2"],
               lw["ln2_g"], lw["ln2_b"], lw["w_pe"], lw["w_pg"], lw["b_pg"])
    return pl.pallas_call(
        functools.partial(_post_kernel, alpha=alpha, ff_chunk=min(1024, lw["w_ff1"].shape[1])),
        out_shape=jax.ShapeDtypeStruct((m, d), F32),
        grid=(m // tm,),
        in_specs=[row(d), row(o.shape[1]), row(y.shape[1]), row(d), row(d), row(pe.shape[1])]
        + [_resident(w.shape) for w in weights],
        out_specs=row(d),
        compiler_params=_cparams(("parallel",)),
        name=name,
    )(x, o, y, ga, gb, pe, *weights)


def _sample_scores_kernel(pt_ref, qp_ref, w_ref, *refs, npg):
    pages, o_ref = refs[:npg], refs[npg]
    kcat = jnp.concatenate([pg[0] for pg in pages], axis=0)
    kh, kl = _split_hi_lo(kcat)
    qp = qp_ref[0]
    q2 = jnp.concatenate([qp[:, 0:IDX_DIM], qp[:, 2 * IDX_DIM:3 * IDX_DIM]], axis=0)
    nt = (((1,), (1,)), ((), ()))
    a = lax.dot_general(q2, kh, nt, preferred_element_type=F32)
    b = lax.dot_general(q2, kl, nt, preferred_element_type=F32)
    s = (a[0:IDX_HEADS] + a[IDX_HEADS:]) + (b[0:IDX_HEADS] + b[IDX_HEADS:])
    o_ref[0] = jnp.sum(w_ref[0] * jnp.maximum(s, 0.0), axis=0, keepdims=True)


def _sample_scores(page_table_flat, qp_s, wi_s, cache_kidx, *, n_pages, npg):
    nb = qp_s.shape[0]
    ngrp = n_pages // npg
    page_specs = [
        pl.BlockSpec((1, PAGE_SIZE, IDX_DIM),
                     functools.partial(lambda b, g, pt, i: (pt[b * n_pages + g * npg + i], 0, 0), i=i))
        for i in range(npg)]
    return pl.pallas_call(
        functools.partial(_sample_scores_kernel, npg=npg),
        out_shape=jax.ShapeDtypeStruct((nb, 1, n_pages * PAGE_SIZE), F32),
        grid_spec=pltpu.PrefetchScalarGridSpec(
            num_scalar_prefetch=1,
            grid=(nb, ngrp),
            in_specs=[pl.BlockSpec((1, IDX_HEADS, 2 * LANES), lambda b, g, pt: (b, 0, 0)),
                      pl.BlockSpec((1, IDX_HEADS, 1), lambda b, g, pt: (b, 0, 0))] + page_specs,
            out_specs=pl.BlockSpec((1, 1, npg * PAGE_SIZE), lambda b, g, pt: (b, 0, g)),
        ),
        compiler_params=_cparams(("parallel", "arbitrary")),
        name="index_scores_sample",
    )(page_table_flat, qp_s, wi_s, *([cache_kidx] * npg))


def _sample_select_kernel(sp_ref, ss_ref, o_ref, s_ref, *, past, k, chunk):
    rows = sp_ref.shape[0]
    ncols = past + LANES
    s_ref[:, 0:past] = sp_ref[...]
    lane = lax.broadcasted_iota(I32, (rows, LANES), 1)
    s_ref[:, past:ncols] = jnp.where(lane == 0, ss_ref[...], -jnp.inf)
    n_adm = jnp.full((rows, 1), past + 1, I32)
    _select_to_bias(s_ref, o_ref, ncols // chunk, n_adm, rows=rows, chunk=chunk, k=k, ncols=ncols)


def _sample_select(scores_past, score_self, *, k):
    nb, past = scores_past.shape
    ncols = past + LANES
    chunk = LANES * math.gcd(ncols // LANES, 5)
    return pl.pallas_call(
        functools.partial(_sample_select_kernel, past=past, k=k, chunk=chunk),
        out_shape=jax.ShapeDtypeStruct((nb, ncols), F32),
        scratch_shapes=[pltpu.VMEM((nb, ncols), F32)],
        compiler_params=pltpu.CompilerParams(vmem_limit_bytes=VMEM_LIMIT),
        name="index_select_sample",
    )(scores_past, score_self)


def _sample_attn_kernel(pt_ref, q_ref, bp_ref, bs_ref, ks_ref, vs_ref, *refs, npg):
    kpages, vpages = refs[:npg], refs[npg:2 * npg]
    o_ref, m_ref, l_ref, acc_ref = refs[2 * npg:]
    g = pl.program_id(1)

    @pl.when(g == 0)
    def _():
        m_ref[...] = jnp.full(m_ref.shape, NEG_BIAS, F32)
        l_ref[...] = jnp.zeros(l_ref.shape, F32)
        acc_ref[...] = jnp.zeros(acc_ref.shape, F32)

    head_of_lane = lax.broadcasted_iota(I32, (N_HEADS, D_ATTN), 1) // HEAD_DIM
    own = head_of_lane == lax.broadcasted_iota(I32, (N_HEADS, D_ATTN), 0)
    qrows = jnp.where(own, q_ref[0], 0.0)
    kcat = jnp.concatenate([pg[0].astype(BF16) for pg in kpages], axis=0)
    vcat = jnp.concatenate([pg[0].astype(BF16) for pg in vpages], axis=0)
    s = lax.dot_general(qrows.astype(BF16), kcat, (((1,), (1,)), ((), ())), preferred_element_type=F32)
    s = s + bp_ref[0]
    m_prev = m_ref[...]
    m_new = jnp.maximum(m_prev, jnp.max(s, axis=1, keepdims=True))
    alpha = jnp.exp(m_prev - m_new)
    pm = jnp.exp(s - m_new)
    l_ref[...] = alpha * l_ref[...] + jnp.sum(pm, axis=1, keepdims=True)
    m_ref[...] = m_new
    acc_ref[...] = alpha * acc_ref[...] + jnp.dot(pm.astype(BF16), vcat, preferred_element_type=F32)

    @pl.when(g == pl.num_programs(1) - 1)
    def _():
        s_self = jnp.sum(qrows.astype(BF16).astype(F32) * ks_ref[0].astype(BF16).astype(F32), axis=1,
                         keepdims=True) + bs_ref[0]
        m_prev = m_ref[...]
        m_new = jnp.maximum(m_prev, s_self)
        alpha = jnp.exp(m_prev - m_new)
        p_self = jnp.exp(s_self - m_new)
        l = alpha * l_ref[...] + p_self
        acc = alpha * acc_ref[...] + p_self.astype(BF16).astype(F32) * vs_ref[0].astype(BF16).astype(F32)
        o_ref[0] = jnp.sum(jnp.where(own, acc / l, 0.0), axis=0, keepdims=True).astype(o_ref.dtype)


def _sample_attention(page_table_flat, q_s, bias_past, bias_self, k_s, v_s, cache_k, cache_v, *, n_pages, npg):
    nb = q_s.shape[0]
    ngrp = n_pages // npg
    page_map = lambda i: functools.partial(
        lambda b, g, pt, i: (pt[b * n_pages + g * npg + i], 0, 0), i=i)
    kv_specs = [pl.BlockSpec((1, PAGE_SIZE, D_ATTN), page_map(i)) for i in range(npg)]
    per_seq = lambda n: pl.BlockSpec((1, 1, n), lambda b, g, pt: (b, 0, 0))
    return pl.pallas_call(
        functools.partial(_sample_attn_kernel, npg=npg),
        out_shape=jax.ShapeDtypeStruct((nb, 1, D_ATTN), BF16),
        grid_spec=pltpu.PrefetchScalarGridSpec(
            num_scalar_prefetch=1,
            grid=(nb, ngrp),
            in_specs=[per_seq(D_ATTN),
                      pl.BlockSpec((1, 1, npg * PAGE_SIZE), lambda b, g, pt: (b, 0, g)),
                      per_seq(1), per_seq(D_ATTN), per_seq(D_ATTN)] + kv_specs + kv_specs,
            out_specs=per_seq(D_ATTN),
            scratch_shapes=[pltpu.VMEM((N_HEADS, 1), F32), pltpu.VMEM((N_HEADS, 1), F32),
                            pltpu.VMEM((N_HEADS, D_ATTN), F32)],
        ),
        compiler_params=_cparams(("parallel", "arbitrary")),
        name="sparse_attention_sample",
    )(page_table_flat, q_s, bias_past, bias_self, k_s, v_s, *([cache_k] * npg), *([cache_v] * npg))


def _pack_layer(l, w_in, b_in, conv_w, conv_b, rg_wa, rg_ba, rg_wi, rg_bi, rg_lam, w_oa, w_ob, w_out,
                ln1_g, ln1_b, w_ff1, w_ff2, ln2_g, ln2_b, w_pe, w_pg, b_pg):
    d_rnn = conv_w.shape[-1]
    w, b = w_in[l], b_in[l]
    n_idx = 4 * D_ATTN
    ki0, wi0, r0 = n_idx, n_idx + IDX_DIM, n_idx + IDX_DIM + IDX_HEADS

    def pad_cols(a, n):
        return jnp.pad(a, ((0, 0), (0, n - a.shape[1])))

    b2 = b[None, :]
    w_head = jnp.concatenate([w[:, :n_idx], pad_cols(w[:, ki0:wi0], LANES), pad_cols(w[:, wi0:r0], LANES)], axis=1)
    b_head = jnp.concatenate([b2[:, :n_idx], pad_cols(b2[:, ki0:wi0], LANES), pad_cols(b2[:, wi0:r0], LANES)], axis=1)
    vec = lambda a: a[l][None, :]
    return dict(
        w_head=w_head.astype(BF16), b_head=b_head, w_rest=w[:, r0:].astype(BF16), b_rest=b2[:, r0:],
        conv_w=conv_w[l], conv_b=vec(conv_b), rg_wa=rg_wa[l].astype(BF16), rg_ba=vec(rg_ba),
        rg_wi=rg_wi[l].astype(BF16), rg_bi=vec(rg_bi), rg_lam=vec(rg_lam),
        w_oa=w_oa[l].astype(BF16), w_ob=w_ob[l].astype(BF16), w_out=w_out[l].astype(BF16),
        ln1_g=vec(ln1_g), ln1_b=vec(ln1_b), w_ff1=w_ff1[l].astype(BF16), w_ff2=w_ff2[l].astype(BF16),
        ln2_g=vec(ln2_g), ln2_b=vec(ln2_b), w_pe=w_pe[l].astype(BF16), w_pg=w_pg[l].astype(BF16), b_pg=vec(b_pg),
        d_rnn=d_rnn)


def kernel(x_prompt, x_sample, p_prompt, p_sample, cache_k, cache_v, cache_kidx, state_h, state_conv, page_table, w_in, b_in, conv_w, conv_b, rg_wa, rg_ba, rg_wi, rg_bi, rg_lam, w_oa, w_ob, w_out, ln1_g, ln1_b, w_ff1, w_ff2, ln2_g, ln2_b, w_pe, w_pg, b_pg):
    batch, seq, d_model = x_prompt.shape
    nb, dec_seq, _ = x_sample.shape
    assert dec_seq == 1
    depth = w_in.shape[0]
    d_rnn = conv_w.shape[-1]
    n_pages = page_table.shape[1]
    past = n_pages * PAGE_SIZE
    alpha = (2.0 * depth) ** 0.25
    k_prompt_sel = min(TOPK_MAX, seq // 4)
    k_sample_sel = min(TOPK_MAX, (past + dec_seq) // 4)
    m = batch * seq
    tm = min(256, m)
    npg = math.gcd(n_pages, 8)
    pt_flat = page_table.reshape(-1).astype(I32)
    n_pool = cache_k.shape[1]

    xp = x_prompt.reshape(m, d_model)
    xs = x_sample.reshape(nb, d_model)
    outs = [[] for _ in range(10)]
    for l in range(depth):
        lw = _pack_layer(l, w_in, b_in, conv_w, conv_b, rg_wa, rg_ba, rg_wi, rg_bi, rg_lam, w_oa, w_ob, w_out,
                         ln1_g, ln1_b, w_ff1, w_ff2, ln2_g, ln2_b, w_pe, w_pg, b_pg)
        rg = (lw["conv_w"], lw["conv_b"], lw["rg_wa"], lw["rg_ba"], lw["rg_wi"], lw["rg_bi"], lw["rg_lam"])

        (q4, kxt, vx, k32, v32, qp, kt, ki32, wi, xr, gr, ga, gb) = _inproj(
            xp, lw["w_head"], lw["b_head"], lw["w_rest"], lw["b_rest"], prompt=True, tm=tm)
        bias = _index_topk(qp, wi, kt, batch=batch, seq=seq, k=k_prompt_sel)
        o_attn = _attention(q4, kxt, vx, bias, batch=batch, seq=seq)
        y_rnn, h_last = _rglru(xr, gr, *rg, batch=batch, seq=seq)
        xp = _post(xp, o_attn, y_rnn, ga, gb, p_prompt[l].reshape(m, -1), lw, alpha=alpha, tm=tm,
                   name="post_prompt")
        outs[0].append(k32.reshape(batch, seq, N_HEADS, HEAD_DIM))
        outs[1].append(v32.reshape(batch, seq, N_HEADS, HEAD_DIM))
        outs[2].append(ki32.reshape(batch, seq, IDX_DIM))
        outs[3].append(h_last.reshape(batch, d_rnn))
        outs[4].append(xr.reshape(batch, seq, d_rnn)[:, seq - (CONV_W - 1):])

        (q_s, k_s, v_s, qp_s, ki_s, wi_s, ss_s, xr_s, gr_s, ga_s, gb_s) = _inproj(
            xs, lw["w_head"], lw["b_head"], lw["w_rest"], lw["b_rest"], prompt=False, tm=nb)
        scores_past = _sample_scores(pt_flat, jnp.swapaxes(qp_s, 0, 1), wi_s.reshape(nb, IDX_HEADS, 1),
                                     cache_kidx[l], n_pages=n_pages, npg=npg)
        bias_s = _sample_select(scores_past.reshape(nb, past), ss_s, k=k_sample_sel)
        o_s = _sample_attention(
            pt_flat, q_s.reshape(nb, 1, D_ATTN), bias_s[:, :past].reshape(nb, 1, past),
            bias_s[:, past:past + 1].reshape(nb, 1, 1), k_s.reshape(nb, 1, D_ATTN), v_s.reshape(nb, 1, D_ATTN),
            cache_k[l].reshape(n_pool, PAGE_SIZE, D_ATTN), cache_v[l].reshape(n_pool, PAGE_SIZE, D_ATTN),
            n_pages=n_pages, npg=npg)
        y_s, h_s = _rglru_step(xr_s, gr_s, jnp.swapaxes(state_conv[l], 0, 1), state_h[l], *rg)
        xs = _post(xs, o_s.reshape(nb, D_ATTN), y_s, ga_s, gb_s, p_sample[l].reshape(nb, -1), lw, alpha=alpha,
                   tm=nb, name="post_sample")
        outs[5].append(k_s.reshape(nb, 1, N_HEADS, HEAD_DIM))
        outs[6].append(v_s.reshape(nb, 1, N_HEADS, HEAD_DIM))
        outs[7].append(ki_s.reshape(nb, 1, IDX_DIM))
        outs[8].append(h_s)
        outs[9].append(jnp.concatenate([state_conv[l][:, 1:], xr_s[:, None, :]], axis=1))

    return (xp.reshape(batch, seq, d_model), xs.reshape(nb, 1, d_model)) + tuple(jnp.stack(o) for o in outs)
```

```python
import functools
import math

import numpy as np
import jax
import jax.numpy as jnp
from jax import lax
from jax.experimental import pallas as pl
from jax.experimental.pallas import tpu as pltpu

F32 = jnp.float32
BF16 = jnp.bfloat16
I32 = jnp.int32

N_HEADS = 8
HEAD_DIM = 64
D_ATTN = N_HEADS * HEAD_DIM
IDX_HEADS = 8
IDX_DIM = 64
TOPK_MAX = 256
PAGE_SIZE = 128
RNN_BLOCKS = 8
CONV_W = 4
RG_C = 8.0
LN_EPS = 1e-5

LANES = 128
VMEM_LIMIT = 56 * 1024 * 1024
NEG_BIAS = -1e30
F32_LOWEST = float(np.finfo(np.float32).min)

_SEG = {}
_off = 0
for _name, _n in (("q", D_ATTN), ("k", D_ATTN), ("v", D_ATTN), ("qi", IDX_HEADS * IDX_DIM),
                  ("ki", LANES), ("wi", LANES)):
    _SEG[_name] = (_off, _off + _n)
    _off += _n


def _cparams(sem):
    return pltpu.CompilerParams(dimension_semantics=sem, vmem_limit_bytes=VMEM_LIMIT)


def _resident(shape):
    nd = len(shape)
    return pl.BlockSpec(shape, lambda *_: (0,) * nd, pipeline_mode=pl.Buffered(1))


def _sigmoid(x):
    return 1.0 / (1.0 + jnp.exp(-x))


def _gelu_tanh(x):
    c = math.sqrt(2.0 / math.pi)
    return 0.5 * x * (1.0 + jnp.tanh(c * (x + 0.044715 * (x * x * x))))


def _expm1(x):
    u = jnp.exp(x)
    um1 = u - 1.0
    return jnp.where(u == 1.0, x, jnp.where(x < -0.5, um1, um1 * x / jnp.log(u)))


def _layer_norm(x, g, b):
    mu = jnp.mean(x, axis=-1, keepdims=True)
    xc = x - mu
    var = jnp.mean(xc * xc, axis=-1, keepdims=True)
    return xc * lax.rsqrt(var + LN_EPS) * g + b


def _split_hi_lo(x):
    hi = x.astype(BF16)
    lo = (x - hi.astype(F32)).astype(BF16)
    return hi, lo


def _swap_halves(t):
    return jnp.concatenate([t[:, HEAD_DIM:], t[:, :HEAD_DIM]], axis=1)


def _inproj_kernel(x_ref, w_ref, b_ref, wr_ref, br_ref, *out_refs, prompt, d_rnn):
    if prompt:
        (qt_ref, kx_ref, vt_ref, k_ref, v_ref, qpt_ref, kp_ref, ki_ref, wit_ref,
         xr_ref, gr_ref, ga_ref, gb_ref) = out_refs
    else:
        (q_ref, k_ref, v_ref, qp_ref, ki_ref, wi_ref, ss_ref,
         xr_ref, gr_ref, ga_ref, gb_ref) = out_refs
    xb = x_ref[...].astype(BF16)
    tm = xb.shape[0]

    def seg(name):
        a, b = _SEG[name]
        return jnp.dot(xb, w_ref[:, a:b], preferred_element_type=F32) + b_ref[:, a:b]

    lane = lax.broadcasted_iota(I32, (tm, LANES), 1)
    low = lane < HEAD_DIM

    zq = seg("q") * (HEAD_DIM ** -0.5)
    if prompt:
        for p in range(N_HEADS // 2):
            qt_ref[p] = zq[:, p * LANES:(p + 1) * LANES].T.astype(BF16)
    else:
        q_ref[...] = zq

    zk = seg("k")
    k_ref[...] = zk
    zv = seg("v")
    v_ref[...] = zv
    if prompt:
        zkb = zk.astype(BF16)
        zero_b = jnp.zeros((tm, LANES), BF16)
        for p in range(N_HEADS // 2):
            kt = zkb[:, p * LANES:(p + 1) * LANES]
            kx_ref[2 * p] = jnp.where(low, kt, zero_b)
            kx_ref[2 * p + 1] = jnp.where(low, zero_b, kt)
            vt_ref[p * LANES:(p + 1) * LANES, :] = zv[:, p * LANES:(p + 1) * LANES].T.astype(BF16)

    zqi = seg("qi") * (IDX_DIM ** -0.5)
    zki = seg("ki")
    ki_ref[...] = zki[:, :IDX_DIM]
    zwi = seg("wi") * (IDX_HEADS ** -0.5)
    kdup = jnp.where(low, zki, _swap_halves(zki))
    if prompt:
        wit_ref[...] = zwi.T[:IDX_HEADS]
        khi, klo = _split_hi_lo(kdup)
        khl = jnp.where(low, khi, klo)
        kp_ref[:, 0:LANES] = khl
        kp_ref[:, LANES:2 * LANES] = khl
    else:
        wi_ref[...] = zwi[:, :IDX_HEADS]
    n_qblk = tm // LANES if prompt else 0
    for p in range(IDX_HEADS // 2):
        t = zqi[:, p * LANES:(p + 1) * LANES]
        sw = _swap_halves(t)
        for e in range(2):
            h = 2 * p + e
            dup = jnp.where(low, t, sw) if e == 0 else jnp.where(low, sw, t)
            if prompt:
                hi, lo = _split_hi_lo(dup.T)
                for r in range(n_qblk):
                    c0 = (r * IDX_HEADS + h) * LANES
                    qpt_ref[0:LANES, c0:c0 + LANES] = hi[:, r * LANES:(r + 1) * LANES]
                    qpt_ref[LANES:2 * LANES, c0:c0 + LANES] = lo[:, r * LANES:(r + 1) * LANES]
            else:
                hi, lo = _split_hi_lo(dup)
                qp_ref[h, :, 0:LANES] = hi
                qp_ref[h, :, LANES:2 * LANES] = lo
    if not prompt:
        acc = jnp.zeros((tm, 1), F32)
        for p in range(IDX_HEADS // 2):
            prod = zqi[:, p * LANES:(p + 1) * LANES] * kdup
            for e in range(2):
                h = 2 * p + e
                keep = low if e == 0 else jnp.logical_not(low)
                s_h = jnp.sum(jnp.where(keep, prod, 0.0), axis=1, keepdims=True)
                acc = acc + zwi[:, h:h + 1] * jnp.maximum(s_h, 0.0)
        ss_ref[...] = acc

    for i, ref in enumerate((xr_ref, gr_ref, ga_ref, gb_ref)):
        a, b = i * d_rnn, (i + 1) * d_rnn
        ref[...] = jnp.dot(xb, wr_ref[:, a:b], preferred_element_type=F32) + br_ref[:, a:b]


def _inproj(x2d, w_head, b_head, w_rest, b_rest, *, prompt, tm):
    m, d = x2d.shape
    d_rnn = w_rest.shape[1] // 4
    sds = jax.ShapeDtypeStruct
    row = lambda n: pl.BlockSpec((tm, n), lambda i: (i, 0))
    col = lambda n: pl.BlockSpec((n, tm), lambda i: (0, i))
    wide = [sds((m, d_rnn), F32)] * 4
    wide_specs = [row(d_rnn)] * 4
    if prompt:
        qpt_cols = IDX_HEADS * LANES * (tm // LANES)
        out_shape = [sds((N_HEADS // 2, LANES, m), BF16), sds((N_HEADS, m, LANES), BF16),
                     sds((D_ATTN, m), BF16), sds((m, D_ATTN), F32), sds((m, D_ATTN), F32),
                     sds((2 * LANES, IDX_HEADS * m), BF16), sds((m, 2 * LANES), BF16),
                     sds((m, IDX_DIM), F32), sds((IDX_HEADS, m), F32)] + wide
        out_specs = [pl.BlockSpec((N_HEADS // 2, LANES, tm), lambda i: (0, 0, i)),
                     pl.BlockSpec((N_HEADS, tm, LANES), lambda i: (0, i, 0)),
                     col(D_ATTN), row(D_ATTN), row(D_ATTN),
                     pl.BlockSpec((2 * LANES, qpt_cols), lambda i: (0, i)), row(2 * LANES),
                     row(IDX_DIM), col(IDX_HEADS)] + wide_specs
    else:
        out_shape = [sds((m, D_ATTN), F32), sds((m, D_ATTN), F32), sds((m, D_ATTN), F32),
                     sds((IDX_HEADS, m, 2 * LANES), BF16), sds((m, IDX_DIM), F32), sds((m, IDX_HEADS), F32),
                     sds((m, 1), F32)] + wide
        out_specs = [row(D_ATTN), row(D_ATTN), row(D_ATTN),
                     pl.BlockSpec((IDX_HEADS, tm, 2 * LANES), lambda i: (0, i, 0)),
                     row(IDX_DIM), row(IDX_HEADS), row(1)] + wide_specs
    return pl.pallas_call(
        functools.partial(_inproj_kernel, prompt=prompt, d_rnn=d_rnn),
        out_shape=out_shape,
        grid=(m // tm,),
        in_specs=[row(d), _resident(w_head.shape), _resident(b_head.shape),
                  _resident(w_rest.shape), _resident(b_rest.shape)],
        out_specs=out_specs,
        compiler_params=_cparams(("parallel",)),
        name="inproj_prompt" if prompt else "inproj_sample",
    )(x2d, w_head, b_head, w_rest, b_rest)


def _ukey_to_f32(u):
    sk = u ^ jnp.int32(-2 ** 31)
    bits = jnp.where(sk >= 0, sk, sk ^ jnp.int32(2 ** 31 - 1))
    return lax.bitcast_convert_type(bits, F32)


def _select_to_bias(s_ref, o_ref, nch, n_adm, *, key_axis, width, chunk, k, nkeys):
    kf = float(k)
    sub = 64 if key_axis == 0 else LANES
    tiles = chunk // sub
    tshape = (sub, width) if key_axis == 0 else (width, sub)
    stat_shape = (1, width) if key_axis == 0 else (width, 1)
    kidx = lax.broadcasted_iota(I32, tshape, key_axis)

    def tile(ref, o):
        return ref.at[pl.ds(o, sub), :] if key_axis == 0 else ref.at[:, pl.ds(o, sub)]

    def count(pred):
        def body(c, acc):
            off = pl.multiple_of(c * chunk, chunk)
            for u in range(tiles):
                o = off + u * sub
                acc = acc + pred(tile(s_ref, o)[...], o)
            return acc
        acc = lax.fori_loop(0, nch, body, jnp.zeros(tshape, F32))
        return jnp.sum(acc, axis=key_axis, keepdims=True)

    def wide(v):
        return jnp.broadcast_to(v, tshape)

    search = n_adm > k

    def bit_cond(st):
        i, _, done = st
        return jnp.logical_and(i < 32, jnp.sum(done) < float(width))

    def bit_body(st):
        i, cur, done = st
        trial = cur | lax.shift_left(jnp.int32(1), 31 - i)
        thr_w = wide(_ukey_to_f32(trial))
        cnt = count(lambda t, o: jnp.where(t >= thr_w, 1.0, 0.0))
        cur = jnp.where(jnp.logical_and(done == 0.0, cnt >= kf), trial, cur)
        done = jnp.where(cnt == kf, 1.0, done)
        return i + 1, cur, done

    _, cur, _ = lax.while_loop(bit_cond, bit_body,
                               (jnp.int32(0), jnp.zeros(stat_shape, I32), jnp.where(search, 0.0, 1.0).astype(F32)))
    thr = jnp.where(search, _ukey_to_f32(cur), F32_LOWEST)
    thr_w = wide(thr)
    cnt_ge = count(lambda t, o: jnp.where(t >= thr_w, 1.0, 0.0))
    tie = jnp.logical_and(search, cnt_ge > kf)
    n_tie = jnp.sum(jnp.where(tie, 1.0, 0.0))

    def write(bias_fn):
        def body(c, carry):
            off = pl.multiple_of(c * chunk, chunk)
            for u in range(tiles):
                o = off + u * sub
                tile(o_ref, o)[...] = bias_fn(tile(s_ref, o)[...], o).astype(o_ref.dtype)
            return carry
        lax.fori_loop(0, nch, body, 0)

    @pl.when(n_tie == 0.0)
    def _():
        write(lambda t, o: jnp.where(t >= thr_w, 0.0, NEG_BIAS))

    @pl.when(n_tie > 0.0)
    def _():
        cnt_gt = count(lambda t, o: jnp.where(t > thr_w, 1.0, 0.0))
        need = jnp.where(search, kf - cnt_gt, float(2 * nkeys))
        nbits = int(nkeys).bit_length()

        def jbody(i, curj):
            trial = curj | lax.shift_left(jnp.int32(1), nbits - 1 - i)
            trial_w = wide(trial)
            g = count(lambda t, o: jnp.where(t == thr_w, jnp.where(kidx + o < trial_w, 1.0, 0.0), 0.0))
            return jnp.where(g < need, trial, curj)

        jmax_w = wide(lax.fori_loop(0, nbits, jbody, jnp.zeros(stat_shape, I32)))
        write(lambda t, o: jnp.where(t == thr_w, jnp.where(kidx + o <= jmax_w, 0.0, NEG_BIAS),
                                     jnp.where(t > thr_w, 0.0, NEG_BIAS)))


def _index_topk_kernel(qpt_ref, wit_ref, kp_ref, o_ref, s_ref, *, seq, cols, schunk, chunk, k):
    j = pl.program_id(1)
    nkeys = j * cols + cols
    nch = (nkeys + chunk - 1) // chunk
    nsc = (nkeys + schunk - 1) // schunk
    w = wit_ref[...]
    qpos = j * cols + lax.broadcasted_iota(I32, (schunk, cols), 1)
    krow = lax.broadcasted_iota(I32, (schunk, cols), 0)

    def score_body(c, carry):
        off = pl.multiple_of(c * schunk, schunk)
        kc = kp_ref[pl.ds(off, schunk), :]
        acc = None
        for pr in range(IDX_HEADS // 2):
            st = jnp.dot(kc, qpt_ref[:, pr * 2 * cols:(pr + 1) * 2 * cols], preferred_element_type=F32)
            for e in range(2):
                h = 2 * pr + e
                term = w[h:h + 1, :] * jnp.maximum(st[:, e * cols:(e + 1) * cols], 0.0)
                acc = term if acc is None else acc + term
        s_ref[pl.ds(off, schunk), :] = jnp.where(krow + off <= qpos, acc, -jnp.inf)
        return carry

    lax.fori_loop(0, nsc, score_body, 0)
    n_adm = j * cols + lax.broadcasted_iota(I32, (1, cols), 1) + 1
    _select_to_bias(s_ref, o_ref, nch, n_adm, key_axis=0, width=cols, chunk=chunk, k=k, nkeys=seq)

    def fill(c, carry):
        off = pl.multiple_of(c * chunk, chunk)
        o_ref[pl.ds(off, chunk), :] = jnp.full((chunk, cols), NEG_BIAS, o_ref.dtype)
        return carry

    lax.fori_loop(nch, seq // chunk, fill, 0)


def _index_topk(qpt, wit, kp, *, batch, seq, k):
    cols = LANES
    chunk = min(512, seq)
    schunk = min(1024, seq)
    nqb = seq // cols
    return pl.pallas_call(
        functools.partial(_index_topk_kernel, seq=seq, cols=cols, schunk=schunk, chunk=chunk, k=k),
        out_shape=jax.ShapeDtypeStruct((batch, seq, seq), BF16),
        grid=(batch, nqb),
        in_specs=[pl.BlockSpec((2 * LANES, IDX_HEADS * cols), lambda b, j: (0, b * nqb + j)),
                  pl.BlockSpec((IDX_HEADS, cols), lambda b, j: (0, b * nqb + j)),
                  pl.BlockSpec((seq, 2 * LANES), lambda b, j: (b, 0))],
        out_specs=pl.BlockSpec((None, seq, cols), lambda b, j: (b, 0, j)),
        scratch_shapes=[pltpu.VMEM((seq, cols), F32)],
        compiler_params=_cparams(("parallel", "arbitrary")),
        name="index_topk_prompt",
    )(qpt, wit, kp)


def _attn_kernel(qt_tab, kt_tab, qt_ref, kx_ref, vt_ref, b_ref, o_ref, m_ref, l_ref, acc_ref, *, tq, tk):
    p = pl.program_id(1)
    kt = kt_tab[p]
    last = (qt_tab[p] * tq + tq - 1) // tk

    @pl.when(kt == 0)
    def _():
        m_ref[...] = jnp.full(m_ref.shape, NEG_BIAS, F32)
        l_ref[...] = jnp.zeros(l_ref.shape, F32)
        acc_ref[...] = jnp.zeros(acc_ref.shape, F32)

    bias = b_ref[...].astype(F32)
    for h in range(N_HEADS):
        rows = slice(h * HEAD_DIM, (h + 1) * HEAD_DIM)
        s = jnp.dot(kx_ref[h], qt_ref[h // 2], preferred_element_type=F32) + bias
        m_prev = m_ref[h:h + 1, :]
        m_new = jnp.maximum(m_prev, jnp.max(s, axis=0, keepdims=True))
        alpha = jnp.exp(m_prev - m_new)
        pm = jnp.exp(s - m_new)
        l_ref[h:h + 1, :] = alpha * l_ref[h:h + 1, :] + jnp.sum(pm, axis=0, keepdims=True)
        m_ref[h:h + 1, :] = m_new
        acc_ref[rows, :] = alpha * acc_ref[rows, :] + jnp.dot(vt_ref[rows, :], pm.astype(BF16),
                                                              preferred_element_type=F32)

    @pl.when(kt == last)
    def _():
        for h in range(N_HEADS):
            rows = slice(h * HEAD_DIM, (h + 1) * HEAD_DIM)
            acc_ref[rows, :] = acc_ref[rows, :] / l_ref[h:h + 1, :]
        o_ref[...] = acc_ref[...].T.astype(o_ref.dtype)


def _attention(qt4, kx, vt, bias, *, batch, seq):
    tq = min(256, seq)
    tk = min(512, seq)
    nq, nk = seq // tq, seq // tk
    qt, kt = [], []
    for i in range(nq):
        for c in range((i * tq + tq - 1) // tk + 1):
            qt.append(i)
            kt.append(c)
    npairs = len(qt)
    m = batch * seq
    return pl.pallas_call(
        functools.partial(_attn_kernel, tq=tq, tk=tk),
        out_shape=jax.ShapeDtypeStruct((m, D_ATTN), BF16),
        grid_spec=pltpu.PrefetchScalarGridSpec(
            num_scalar_prefetch=2,
            grid=(batch, npairs),
            in_specs=[
                pl.BlockSpec((N_HEADS // 2, LANES, tq), lambda b, p, qt, kt: (0, 0, b * nq + qt[p])),
                pl.BlockSpec((N_HEADS, tk, LANES), lambda b, p, qt, kt: (0, b * nk + kt[p], 0)),
                pl.BlockSpec((D_ATTN, tk), lambda b, p, qt, kt: (0, b * nk + kt[p])),
                pl.BlockSpec((None, tk, tq), lambda b, p, qt, kt: (b, kt[p], qt[p])),
            ],
            out_specs=pl.BlockSpec((tq, D_ATTN), lambda b, p, qt, kt: (b * nq + qt[p], 0)),
            scratch_shapes=[pltpu.VMEM((N_HEADS, tq), F32), pltpu.VMEM((N_HEADS, tq), F32),
                            pltpu.VMEM((D_ATTN, tq), F32)],
        ),
        compiler_params=_cparams(("parallel", "arbitrary")),
        name="sparse_attention_prompt",
    )(jnp.asarray(qt, I32), jnp.asarray(kt, I32), qt4, kx, vt, bias)


def _rglru_gates(xc, wa_ref, ba_ref, wg_ref, bg_ref, lam_ref, store):
    xcb = xc.astype(BF16)
    lam = lam_ref[...]
    nlam = -lam
    softplus = jnp.maximum(nlam, 0.0) + jnp.log1p(jnp.exp(-jnp.abs(nlam)))
    bw = xc.shape[1] // RNN_BLOCKS
    for n in range(RNN_BLOCKS):
        sl = slice(n * bw, (n + 1) * bw)
        r = _sigmoid(jnp.dot(xcb[:, sl], wa_ref[n], preferred_element_type=F32) + ba_ref[:, sl])
        ig = _sigmoid(jnp.dot(xcb[:, sl], wg_ref[n], preferred_element_type=F32) + bg_ref[:, sl])
        log_a = (-RG_C) * r * softplus[:, sl]
        a = jnp.exp(log_a)
        bx = jnp.sqrt(-_expm1(2.0 * log_a)) * (ig * xc[:, sl])
        store(sl, a, bx)


def _rglru_kernel(xr_ref, gr_ref, cw_ref, cb_ref, wa_ref, ba_ref, wg_ref, bg_ref, lam_ref,
                  y_ref, hl_ref, xbuf, a_s, b_s, h_s, *, tc):
    c = pl.program_id(1)
    pad = 8

    @pl.when(c == 0)
    def _():
        xbuf[0:pad] = jnp.zeros((pad, xbuf.shape[1]), F32)
        h_s[...] = jnp.zeros(h_s.shape, F32)

    x = xr_ref[...]
    xbuf[pad:pad + tc] = x
    cw = cw_ref[...]
    xc = cb_ref[...] + cw[CONV_W - 1:CONV_W] * x
    for jj in range(1, CONV_W):
        xc = xc + cw[CONV_W - 1 - jj:CONV_W - jj] * xbuf[pad - jj:pad - jj + tc]
    xbuf[0:pad] = x[tc - pad:tc]

    def store(sl, a, bx):
        a_s[:, sl] = a
        b_s[:, sl] = bx

    _rglru_gates(xc, wa_ref, ba_ref, wg_ref, bg_ref, lam_ref, store)

    def step(t, h):
        h = a_s[pl.ds(t, 1), :] * h + b_s[pl.ds(t, 1), :]
        b_s[pl.ds(t, 1), :] = h
        return h

    h = lax.fori_loop(0, tc, step, h_s[...], unroll=8)
    h_s[...] = h
    hl_ref[0] = h
    y_ref[...] = (b_s[...] * _gelu_tanh(gr_ref[...])).astype(y_ref.dtype)


def _rglru(xr, gr, cw, cb, wa, ba, wg, bg, lam, *, batch, seq):
    tc = min(512, seq)
    nc = seq // tc
    m, c = xr.shape
    row = pl.BlockSpec((tc, c), lambda b, i: (b * nc + i, 0))
    vec = pl.BlockSpec((1, c), lambda b, i: (0, 0))
    blk = pl.BlockSpec(wa.shape, lambda b, i: (0, 0, 0))
    return pl.pallas_call(
        functools.partial(_rglru_kernel, tc=tc),
        out_shape=[jax.ShapeDtypeStruct((m, c), BF16), jax.ShapeDtypeStruct((batch, 1, c), F32)],
        grid=(batch, nc),
        in_specs=[row, row, pl.BlockSpec((CONV_W, c), lambda b, i: (0, 0)), vec, blk, vec, blk, vec, vec],
        out_specs=[row, pl.BlockSpec((1, 1, c), lambda b, i: (b, 0, 0))],
        scratch_shapes=[pltpu.VMEM((tc + 8, c), F32), pltpu.VMEM((tc, c), F32), pltpu.VMEM((tc, c), F32),
                        pltpu.VMEM((1, c), F32)],
        compiler_params=_cparams(("parallel", "arbitrary")),
        name="rglru_prompt",
    )(xr, gr, cw, cb, wa, ba, wg, bg, lam)


def _rglru_step_kernel(xr_ref, gr_ref, sc_ref, h0_ref, cw_ref, cb_ref, wa_ref, ba_ref, wg_ref, bg_ref, lam_ref,
                       y_ref, h_ref):
    x = xr_ref[...]
    cw = cw_ref[...]
    xc = cb_ref[...] + cw[CONV_W - 1:CONV_W] * x
    for jj in range(CONV_W - 1):
        xc = xc + cw[jj:jj + 1] * sc_ref[jj]
    h0 = h0_ref[...]

    def store(sl, a, bx):
        h_ref[:, sl] = a * h0[:, sl] + bx

    _rglru_gates(xc, wa_ref, ba_ref, wg_ref, bg_ref, lam_ref, store)
    y_ref[...] = (h_ref[...] * _gelu_tanh(gr_ref[...])).astype(y_ref.dtype)


def _rglru_step(xr, gr, sc, h0, cw, cb, wa, ba, wg, bg, lam):
    m, c = xr.shape
    return pl.pallas_call(
        _rglru_step_kernel,
        out_shape=[jax.ShapeDtypeStruct((m, c), BF16), jax.ShapeDtypeStruct((m, c), F32)],
        compiler_params=pltpu.CompilerParams(vmem_limit_bytes=VMEM_LIMIT),
        name="rglru_sample",
    )(xr, gr, sc, h0, cw, cb, wa, ba, wg, bg, lam)


def _post_kernel(x_ref, o_ref, y_ref, ga_ref, gb_ref, pe_ref,
                 woa_ref, wob_ref, wout_ref, g1_ref, b1_ref, wf1_ref, wf2_ref, g2_ref, b2_ref,
                 wpe_ref, wpg_ref, bpg_ref, out_ref, *, alpha, ff_chunk):
    dot = functools.partial(jnp.dot, preferred_element_type=F32)
    merged = (_sigmoid(ga_ref[...]) * dot(o_ref[...], woa_ref[...])
              + _sigmoid(gb_ref[...]) * dot(y_ref[...], wob_ref[...]))
    x1 = _layer_norm(alpha * x_ref[...] + dot(merged.astype(BF16), wout_ref[...]), g1_ref[...], b1_ref[...])
    x1b = x1.astype(BF16)
    d_ff = wf1_ref.shape[1]
    ff = None
    for c0 in range(0, d_ff, ff_chunk):
        hcol = jnp.maximum(dot(x1b, wf1_ref[:, c0:c0 + ff_chunk]), 0.0)
        part = dot((hcol * hcol).astype(BF16), wf2_ref[c0:c0 + ff_chunk, :])
        ff = part if ff is None else ff + part
    x2 = _layer_norm(alpha * x1 + ff, g2_ref[...], b2_ref[...])
    gate = _sigmoid(dot(x2.astype(BF16), wpg_ref[...]) + bpg_ref[...])
    out_ref[...] = x2 + gate * dot(pe_ref[...].astype(BF16), wpe_ref[...])


def _post(x, o, y, ga, gb, pe, lw, *, alpha, tm, name):
    m, d = x.shape
    row = lambda n: pl.BlockSpec((tm, n), lambda i: (i, 0))
    weights = (lw["w_oa"], lw["w_ob"], lw["w_out"], lw["ln1_g"], lw["ln1_b"], lw["w_ff1"], lw["w_ff2"],
               lw["ln2_g"], lw["ln2_b"], lw["w_pe"], lw["w_pg"], lw["b_pg"])
    return pl.pallas_call(
        functools.partial(_post_kernel, alpha=alpha, ff_chunk=min(1024, lw["w_ff1"].shape[1])),
        out_shape=jax.ShapeDtypeStruct((m, d), F32),
        grid=(m // tm,),
        in_specs=[row(d), row(o.shape[1]), row(y.shape[1]), row(d), row(d), row(pe.shape[1])]
        + [_resident(w.shape) for w in weights],
        out_specs=row(d),
        compiler_params=_cparams(("parallel",)),
        name=name,
    )(x, o, y, ga, gb, pe, *weights)


def _page_specs(block, layer, n_pages, npg):
    nd = len(block)

    def index_map(b, g, pt, i):
        return (layer, pt[b * n_pages + g * npg + i]) + (0,) * (nd - 2)

    return [pl.BlockSpec(block, functools.partial(index_map, i=i)) for i in range(npg)]


def _sample_scores_kernel(pt_ref, qp_ref, w_ref, *refs, npg):
    pages, o_ref = refs[:npg], refs[npg]
    qp = qp_ref[0]
    q2 = jnp.concatenate([qp[:, 0:IDX_DIM], qp[:, 2 * IDX_DIM:3 * IDX_DIM]], axis=0)
    w = w_ref[0]
    for i, pg in enumerate(pages):
        kh, kl = _split_hi_lo(pg[...])
        ab = jnp.dot(q2, kh, preferred_element_type=F32) + jnp.dot(q2, kl, preferred_element_type=F32)
        s = ab[0:IDX_HEADS] + ab[IDX_HEADS:]
        o_ref[0, :, i * PAGE_SIZE:(i + 1) * PAGE_SIZE] = jnp.sum(w * jnp.maximum(s, 0.0), axis=0, keepdims=True)


def _sample_scores(page_table_flat, qp_s, wi_s, cache_kidx_t, layer, *, n_pages, npg):
    nb = qp_s.shape[0]
    ngrp = n_pages // npg
    return pl.pallas_call(
        functools.partial(_sample_scores_kernel, npg=npg),
        out_shape=jax.ShapeDtypeStruct((nb, 1, n_pages * PAGE_SIZE), F32),
        grid_spec=pltpu.PrefetchScalarGridSpec(
            num_scalar_prefetch=1,
            grid=(nb, ngrp),
            in_specs=[pl.BlockSpec((1, IDX_HEADS, 2 * LANES), lambda b, g, pt: (b, 0, 0)),
                      pl.BlockSpec((1, IDX_HEADS, 1), lambda b, g, pt: (b, 0, 0))]
            + _page_specs((None, None, IDX_DIM, PAGE_SIZE), layer, n_pages, npg),
            out_specs=pl.BlockSpec((1, 1, npg * PAGE_SIZE), lambda b, g, pt: (b, 0, g)),
        ),
        compiler_params=_cparams(("parallel", "arbitrary")),
        name="index_scores_sample",
    )(page_table_flat, qp_s, wi_s, *([cache_kidx_t] * npg))


def _sample_select_kernel(sp_ref, ss_ref, o_ref, s_ref, *, past, k, chunk):
    rows = sp_ref.shape[0]
    ncols = past + LANES
    s_ref[:, 0:past] = sp_ref[...]
    lane = lax.broadcasted_iota(I32, (rows, LANES), 1)
    s_ref[:, past:ncols] = jnp.where(lane == 0, ss_ref[...], -jnp.inf)
    n_adm = jnp.full((rows, 1), past + 1, I32)
    _select_to_bias(s_ref, o_ref, ncols // chunk, n_adm, key_axis=1, width=rows, chunk=chunk, k=k, nkeys=ncols)


def _sample_select(scores_past, score_self, *, k):
    nb, past = scores_past.shape
    ncols = past + LANES
    chunk = LANES * math.gcd(ncols // LANES, 5)
    return pl.pallas_call(
        functools.partial(_sample_select_kernel, past=past, k=k, chunk=chunk),
        out_shape=jax.ShapeDtypeStruct((nb, ncols), F32),
        scratch_shapes=[pltpu.VMEM((nb, ncols), F32)],
        compiler_params=pltpu.CompilerParams(vmem_limit_bytes=VMEM_LIMIT),
        name="index_select_sample",
    )(scores_past, score_self)


def _sample_attn_kernel(pt_ref, q_ref, bp_ref, bs_ref, ks_ref, vs_ref, *refs, npg):
    kpages, vpages = refs[:npg], refs[npg:2 * npg]
    o_ref, m_ref, l_ref, acc_ref = refs[2 * npg:]
    g = pl.program_id(1)

    @pl.when(g == 0)
    def _():
        m_ref[...] = jnp.full(m_ref.shape, NEG_BIAS, F32)
        l_ref[...] = jnp.zeros(l_ref.shape, F32)
        acc_ref[...] = jnp.zeros(acc_ref.shape, F32)

    head_of_lane = lax.broadcasted_iota(I32, (N_HEADS, D_ATTN), 1) // HEAD_DIM
    own = head_of_lane == lax.broadcasted_iota(I32, (N_HEADS, D_ATTN), 0)
    qrows = jnp.where(own, q_ref[0], 0.0)
    qb = qrows.astype(BF16)
    flat = lambda pg: pg[...].reshape(D_ATTN, PAGE_SIZE).astype(BF16)
    s = jnp.concatenate([jnp.dot(qb, flat(pg), preferred_element_type=F32) for pg in kpages], axis=1)
    s = s + bp_ref[0]
    m_prev = m_ref[...]
    m_new = jnp.maximum(m_prev, jnp.max(s, axis=1, keepdims=True))
    alpha = jnp.exp(m_prev - m_new)
    pm = jnp.exp(s - m_new)
    l_ref[...] = alpha * l_ref[...] + jnp.sum(pm, axis=1, keepdims=True)
    m_ref[...] = m_new
    pmb = pm.astype(BF16)
    nt = (((1,), (1,)), ((), ()))
    pv = None
    for i, pg in enumerate(vpages):
        part = lax.dot_general(pmb[:, i * PAGE_SIZE:(i + 1) * PAGE_SIZE], flat(pg), nt, preferred_element_type=F32)
        pv = part if pv is None else pv + part
    acc_ref[...] = alpha * acc_ref[...] + pv

    @pl.when(g == pl.num_programs(1) - 1)
    def _():
        rnd = lambda x: x.astype(BF16).astype(F32)
        s_self = jnp.sum(rnd(qrows) * rnd(ks_ref[0]), axis=1, keepdims=True) + bs_ref[0]
        m_prev = m_ref[...]
        m_new = jnp.maximum(m_prev, s_self)
        alpha = jnp.exp(m_prev - m_new)
        p_self = jnp.exp(s_self - m_new)
        l = alpha * l_ref[...] + p_self
        acc = alpha * acc_ref[...] + rnd(p_self) * rnd(vs_ref[0])
        o_ref[0] = jnp.sum(jnp.where(own, acc / l, 0.0), axis=0, keepdims=True).astype(o_ref.dtype)


def _sample_attention(page_table_flat, q_s, bias_past, bias_self, k_s, v_s, cache_k_t, cache_v_t, layer, *,
                      n_pages, npg):
    nb = q_s.shape[0]
    ngrp = n_pages // npg
    kv_specs = _page_specs((None, None, N_HEADS, HEAD_DIM, PAGE_SIZE), layer, n_pages, npg)
    per_seq = lambda n: pl.BlockSpec((1, 1, n), lambda b, g, pt: (b, 0, 0))
    return pl.pallas_call(
        functools.partial(_sample_attn_kernel, npg=npg),
        out_shape=jax.ShapeDtypeStruct((nb, 1, D_ATTN), BF16),
        grid_spec=pltpu.PrefetchScalarGridSpec(
            num_scalar_prefetch=1,
            grid=(nb, ngrp),
            in_specs=[per_seq(D_ATTN),
                      pl.BlockSpec((1, 1, npg * PAGE_SIZE), lambda b, g, pt: (b, 0, g)),
                      per_seq(1), per_seq(D_ATTN), per_seq(D_ATTN)] + kv_specs + kv_specs,
            out_specs=per_seq(D_ATTN),
            scratch_shapes=[pltpu.VMEM((N_HEADS, 1), F32), pltpu.VMEM((N_HEADS, 1), F32),
                            pltpu.VMEM((N_HEADS, D_ATTN), F32)],
        ),
        compiler_params=_cparams(("parallel", "arbitrary")),
        name="sparse_attention_sample",
    )(page_table_flat, q_s, bias_past, bias_self, k_s, v_s, *([cache_k_t] * npg), *([cache_v_t] * npg))


def _pack_layer(l, w_in, b_in, conv_w, conv_b, rg_wa, rg_ba, rg_wi, rg_bi, rg_lam, w_oa, w_ob, w_out,
                ln1_g, ln1_b, w_ff1, w_ff2, ln2_g, ln2_b, w_pe, w_pg, b_pg):
    d_rnn = conv_w.shape[-1]
    w, b = w_in[l], b_in[l]
    n_idx = 4 * D_ATTN
    ki0, wi0, r0 = n_idx, n_idx + IDX_DIM, n_idx + IDX_DIM + IDX_HEADS

    def pad_cols(a, n):
        return jnp.pad(a, ((0, 0), (0, n - a.shape[1])))

    b2 = b[None, :]
    w_head = jnp.concatenate([w[:, :n_idx], pad_cols(w[:, ki0:wi0], LANES), pad_cols(w[:, wi0:r0], LANES)], axis=1)
    b_head = jnp.concatenate([b2[:, :n_idx], pad_cols(b2[:, ki0:wi0], LANES), pad_cols(b2[:, wi0:r0], LANES)], axis=1)
    vec = lambda a: a[l][None, :]
    return dict(
        w_head=w_head.astype(BF16), b_head=b_head, w_rest=w[:, r0:].astype(BF16), b_rest=b2[:, r0:],
        conv_w=conv_w[l], conv_b=vec(conv_b), rg_wa=rg_wa[l].astype(BF16), rg_ba=vec(rg_ba),
        rg_wi=rg_wi[l].astype(BF16), rg_bi=vec(rg_bi), rg_lam=vec(rg_lam),
        w_oa=w_oa[l].astype(BF16), w_ob=w_ob[l].astype(BF16), w_out=w_out[l].astype(BF16),
        ln1_g=vec(ln1_g), ln1_b=vec(ln1_b), w_ff1=w_ff1[l].astype(BF16), w_ff2=w_ff2[l].astype(BF16),
        ln2_g=vec(ln2_g), ln2_b=vec(ln2_b), w_pe=w_pe[l].astype(BF16), w_pg=w_pg[l].astype(BF16), b_pg=vec(b_pg),
        d_rnn=d_rnn)


def kernel(x_prompt, x_sample, p_prompt, p_sample, cache_k, cache_v, cache_kidx, state_h, state_conv, page_table, w_in, b_in, conv_w, conv_b, rg_wa, rg_ba, rg_wi, rg_bi, rg_lam, w_oa, w_ob, w_out, ln1_g, ln1_b, w_ff1, w_ff2, ln2_g, ln2_b, w_pe, w_pg, b_pg):
    batch, seq, d_model = x_prompt.shape
    nb, dec_seq, _ = x_sample.shape
    assert dec_seq == 1
    depth = w_in.shape[0]
    d_rnn = conv_w.shape[-1]
    n_pages = page_table.shape[1]
    past = n_pages * PAGE_SIZE
    alpha = (2.0 * depth) ** 0.25
    k_prompt_sel = min(TOPK_MAX, seq // 4)
    k_sample_sel = min(TOPK_MAX, (past + dec_seq) // 4)
    m = batch * seq
    tm = min(256, m)
    npg = math.gcd(n_pages, 8)
    pt_flat = page_table.reshape(-1).astype(I32)
    cache_k_t = jnp.transpose(cache_k, (0, 1, 3, 4, 2))
    cache_v_t = jnp.transpose(cache_v, (0, 1, 3, 4, 2))
    cache_kidx_t = jnp.transpose(cache_kidx, (0, 1, 3, 2))

    xp = x_prompt.reshape(m, d_model)
    xs = x_sample.reshape(nb, d_model)
    outs = [[] for _ in range(10)]
    for l in range(depth):
        lw = _pack_layer(l, w_in, b_in, conv_w, conv_b, rg_wa, rg_ba, rg_wi, rg_bi, rg_lam, w_oa, w_ob, w_out,
                         ln1_g, ln1_b, w_ff1, w_ff2, ln2_g, ln2_b, w_pe, w_pg, b_pg)
        rg = (lw["conv_w"], lw["conv_b"], lw["rg_wa"], lw["rg_ba"], lw["rg_wi"], lw["rg_bi"], lw["rg_lam"])

        (qt4, kx, vt, k32, v32, qpt, kp, ki32, wit, xr, gr, ga, gb) = _inproj(
            xp, lw["w_head"], lw["b_head"], lw["w_rest"], lw["b_rest"], prompt=True, tm=tm)
        bias = _index_topk(qpt, wit, kp, batch=batch, seq=seq, k=k_prompt_sel)
        o_attn = _attention(qt4, kx, vt, bias, batch=batch, seq=seq)
        y_rnn, h_last = _rglru(xr, gr, *rg, batch=batch, seq=seq)
        xp = _post(xp, o_attn, y_rnn, ga, gb, p_prompt[l].reshape(m, -1), lw, alpha=alpha, tm=tm,
                   name="post_prompt")
        outs[0].append(k32.reshape(batch, seq, N_HEADS, HEAD_DIM))
        outs[1].append(v32.reshape(batch, seq, N_HEADS, HEAD_DIM))
        outs[2].append(ki32.reshape(batch, seq, IDX_DIM))
        outs[3].append(h_last.reshape(batch, d_rnn))
        outs[4].append(xr.reshape(batch, seq, d_rnn)[:, seq - (CONV_W - 1):])

        (q_s, k_s, v_s, qp_s, ki_s, wi_s, ss_s, xr_s, gr_s, ga_s, gb_s) = _inproj(
            xs, lw["w_head"], lw["b_head"], lw["w_rest"], lw["b_rest"], prompt=False, tm=nb)
        scores_past = _sample_scores(pt_flat, jnp.swapaxes(qp_s, 0, 1), wi_s.reshape(nb, IDX_HEADS, 1),
                                     cache_kidx_t, l, n_pages=n_pages, npg=npg)
        bias_s = _sample_select(scores_past.reshape(nb, past), ss_s, k=k_sample_sel)
        o_s = _sample_attention(
            pt_flat, q_s.reshape(nb, 1, D_ATTN), bias_s[:, :past].reshape(nb, 1, past),
            bias_s[:, past:past + 1].reshape(nb, 1, 1), k_s.reshape(nb, 1, D_ATTN), v_s.reshape(nb, 1, D_ATTN),
            cache_k_t, cache_v_t, l, n_pages=n_pages, npg=npg)
        y_s, h_s = _rglru_step(xr_s, gr_s, jnp.swapaxes(state_conv[l], 0, 1), state_h[l], *rg)
        xs = _post(xs, o_s.reshape(nb, D_ATTN), y_s, ga_s, gb_s, p_sample[l].reshape(nb, -1), lw, alpha=alpha,
                   tm=nb, name="post_sample")
        outs[5].append(k_s.reshape(nb, 1, N_HEADS, HEAD_DIM))
        outs[6].append(v_s.reshape(nb, 1, N_HEADS, HEAD_DIM))
        outs[7].append(ki_s.reshape(nb, 1, IDX_DIM))
        outs[8].append(h_s)
        outs[9].append(jnp.concatenate([state_conv[l][:, 1:], xr_s[:, None, :]], axis=1))

    return (xp.reshape(batch, seq, d_model), xs.reshape(nb, 1, d_model)) + tuple(jnp.stack(o) for o in outs)
```

```python
import functools
import math

import numpy as np
import jax
import jax.numpy as jnp
from jax import lax
from jax.experimental import pallas as pl
from jax.experimental.pallas import tpu as pltpu

F32 = jnp.float32
BF16 = jnp.bfloat16
I32 = jnp.int32

N_HEADS = 8
HEAD_DIM = 64
D_ATTN = N_HEADS * HEAD_DIM
IDX_HEADS = 8
IDX_DIM = 64
TOPK_MAX = 256
PAGE_SIZE = 128
RNN_BLOCKS = 8
CONV_W = 4
RG_C = 8.0
LN_EPS = 1e-5

LANES = 128
VMEM_LIMIT = 56 * 1024 * 1024
NEG_BIAS = -1e30
QUERY_BLOCK = 256
LOG2E = math.log2(math.e)
F32_LOWEST = float(np.finfo(np.float32).min)

_SEG = {}
_off = 0
for _name, _n in (("q", D_ATTN), ("k", D_ATTN), ("v", D_ATTN), ("qi", IDX_HEADS * IDX_DIM),
                  ("ki", LANES), ("wi", LANES)):
    _SEG[_name] = (_off, _off + _n)
    _off += _n


def _cparams(sem):
    return pltpu.CompilerParams(dimension_semantics=sem, vmem_limit_bytes=VMEM_LIMIT)


def _resident(shape):
    nd = len(shape)
    return pl.BlockSpec(shape, lambda *_: (0,) * nd, pipeline_mode=pl.Buffered(1))


def _sigmoid(x):
    return 1.0 / (1.0 + jnp.exp(-x))


def _gelu_tanh(x):
    c = math.sqrt(2.0 / math.pi)
    return 0.5 * x * (1.0 + jnp.tanh(c * (x + 0.044715 * (x * x * x))))


def _expm1(x):
    u = jnp.exp(x)
    um1 = u - 1.0
    return jnp.where(u == 1.0, x, jnp.where(x < -0.5, um1, um1 * x / jnp.log(u)))


def _layer_norm(x, g, b):
    mu = jnp.mean(x, axis=-1, keepdims=True)
    xc = x - mu
    var = jnp.mean(xc * xc, axis=-1, keepdims=True)
    return xc * lax.rsqrt(var + LN_EPS) * g + b


def _split_hi_lo(x):
    hi = x.astype(BF16)
    lo = (x - hi.astype(F32)).astype(BF16)
    return hi, lo


def _swap_halves(t):
    return jnp.concatenate([t[:, HEAD_DIM:], t[:, :HEAD_DIM]], axis=1)


def _inproj_kernel(x_ref, w_ref, b_ref, wr_ref, br_ref, *out_refs, prompt, d_rnn):
    if prompt:
        (qt_ref, kx_ref, vt_ref, k_ref, v_ref, qpt_ref, kp_ref, ki_ref, wit_ref,
         xr_ref, gr_ref, ga_ref, gb_ref) = out_refs
    else:
        (q_ref, k_ref, v_ref, qp_ref, ki_ref, wi_ref, ss_ref,
         xr_ref, gr_ref, ga_ref, gb_ref) = out_refs
    xb = x_ref[...].astype(BF16)
    tm = xb.shape[0]

    def seg(name):
        a, b = _SEG[name]
        return jnp.dot(xb, w_ref[:, a:b], preferred_element_type=F32) + b_ref[:, a:b]

    lane = lax.broadcasted_iota(I32, (tm, LANES), 1)
    low = lane < HEAD_DIM

    zq = seg("q") * (HEAD_DIM ** -0.5 * (LOG2E if prompt else 1.0))
    if prompt:
        for p in range(N_HEADS // 2):
            qt_ref[p] = zq[:, p * LANES:(p + 1) * LANES].T.astype(BF16)
    else:
        q_ref[...] = zq

    zk = seg("k")
    k_ref[...] = zk
    zv = seg("v")
    v_ref[...] = zv
    if prompt:
        zkb = zk.astype(BF16)
        zero_b = jnp.zeros((tm, LANES), BF16)
        for p in range(N_HEADS // 2):
            kt = zkb[:, p * LANES:(p + 1) * LANES]
            kx_ref[2 * p] = jnp.where(low, kt, zero_b)
            kx_ref[2 * p + 1] = jnp.where(low, zero_b, kt)
            vt_ref[p * LANES:(p + 1) * LANES, :] = zv[:, p * LANES:(p + 1) * LANES].T.astype(BF16)

    zqi = seg("qi") * (IDX_DIM ** -0.5)
    zki = seg("ki")
    ki_ref[...] = zki[:, :IDX_DIM]
    zwi = seg("wi") * (IDX_HEADS ** -0.5)
    kdup = jnp.where(low, zki, _swap_halves(zki))
    if prompt:
        wit_ref[...] = zwi.T[:IDX_HEADS]
        khi, klo = _split_hi_lo(kdup)
        khl = jnp.where(low, khi, klo)
        kp_ref[:, 0:LANES] = khl
        kp_ref[:, LANES:2 * LANES] = khl
    else:
        wi_ref[...] = zwi[:, :IDX_HEADS]
    qblk = min(QUERY_BLOCK, tm)
    n_qblk = tm // qblk if prompt else 0
    for p in range(IDX_HEADS // 2):
        t = zqi[:, p * LANES:(p + 1) * LANES]
        sw = _swap_halves(t)
        for e in range(2):
            h = 2 * p + e
            dup = jnp.where(low, t, sw) if e == 0 else jnp.where(low, sw, t)
            if prompt:
                hi, lo = _split_hi_lo(dup.T)
                for r in range(n_qblk):
                    c0 = (r * IDX_HEADS + h) * qblk
                    qpt_ref[0:LANES, c0:c0 + qblk] = hi[:, r * qblk:(r + 1) * qblk]
                    qpt_ref[LANES:2 * LANES, c0:c0 + qblk] = lo[:, r * qblk:(r + 1) * qblk]
            else:
                hi, lo = _split_hi_lo(dup)
                qp_ref[h, :, 0:LANES] = hi
                qp_ref[h, :, LANES:2 * LANES] = lo
    if not prompt:
        acc = jnp.zeros((tm, 1), F32)
        for p in range(IDX_HEADS // 2):
            prod = zqi[:, p * LANES:(p + 1) * LANES] * kdup
            for e in range(2):
                h = 2 * p + e
                keep = low if e == 0 else jnp.logical_not(low)
                s_h = jnp.sum(jnp.where(keep, prod, 0.0), axis=1, keepdims=True)
                acc = acc + zwi[:, h:h + 1] * jnp.maximum(s_h, 0.0)
        ss_ref[...] = acc

    for i, ref in enumerate((xr_ref, gr_ref, ga_ref, gb_ref)):
        a, b = i * d_rnn, (i + 1) * d_rnn
        ref[...] = jnp.dot(xb, wr_ref[:, a:b], preferred_element_type=F32) + br_ref[:, a:b]


def _inproj(x2d, w_head, b_head, w_rest, b_rest, *, prompt, tm):
    m, d = x2d.shape
    d_rnn = w_rest.shape[1] // 4
    sds = jax.ShapeDtypeStruct
    row = lambda n: pl.BlockSpec((tm, n), lambda i: (i, 0))
    col = lambda n: pl.BlockSpec((n, tm), lambda i: (0, i))
    wide = [sds((m, d_rnn), F32)] * 4
    wide_specs = [row(d_rnn)] * 4
    if prompt:
        qpt_cols = IDX_HEADS * tm
        out_shape = [sds((N_HEADS // 2, LANES, m), BF16), sds((N_HEADS, m, LANES), BF16),
                     sds((D_ATTN, m), BF16), sds((m, D_ATTN), F32), sds((m, D_ATTN), F32),
                     sds((2 * LANES, IDX_HEADS * m), BF16), sds((m, 2 * LANES), BF16),
                     sds((m, IDX_DIM), F32), sds((IDX_HEADS, m), F32)] + wide
        out_specs = [pl.BlockSpec((N_HEADS // 2, LANES, tm), lambda i: (0, 0, i)),
                     pl.BlockSpec((N_HEADS, tm, LANES), lambda i: (0, i, 0)),
                     col(D_ATTN), row(D_ATTN), row(D_ATTN),
                     pl.BlockSpec((2 * LANES, qpt_cols), lambda i: (0, i)), row(2 * LANES),
                     row(IDX_DIM), col(IDX_HEADS)] + wide_specs
    else:
        out_shape = [sds((m, D_ATTN), F32), sds((m, D_ATTN), F32), sds((m, D_ATTN), F32),
                     sds((IDX_HEADS, m, 2 * LANES), BF16), sds((m, IDX_DIM), F32), sds((m, IDX_HEADS), F32),
                     sds((m, 1), F32)] + wide
        out_specs = [row(D_ATTN), row(D_ATTN), row(D_ATTN),
                     pl.BlockSpec((IDX_HEADS, tm, 2 * LANES), lambda i: (0, i, 0)),
                     row(IDX_DIM), row(IDX_HEADS), row(1)] + wide_specs
    return pl.pallas_call(
        functools.partial(_inproj_kernel, prompt=prompt, d_rnn=d_rnn),
        out_shape=out_shape,
        grid=(m // tm,),
        in_specs=[row(d), _resident(w_head.shape), _resident(b_head.shape),
                  _resident(w_rest.shape), _resident(b_rest.shape)],
        out_specs=out_specs,
        compiler_params=_cparams(("parallel",)),
        name="inproj_prompt" if prompt else "inproj_sample",
    )(x2d, w_head, b_head, w_rest, b_rest)


def _ukey_to_f32(u):
    sk = u ^ jnp.int32(-2 ** 31)
    bits = jnp.where(sk >= 0, sk, sk ^ jnp.int32(2 ** 31 - 1))
    return lax.bitcast_convert_type(bits, F32)


def _select_to_bias(s_ref, o_ref, nch, n_adm, *, key_axis, width, chunk, k, nkeys):
    kf = float(k)
    sub = 64 if key_axis == 0 else LANES
    tiles = chunk // sub
    tshape = (sub, width) if key_axis == 0 else (width, sub)
    stat_shape = (1, width) if key_axis == 0 else (width, 1)

    def tile(ref, o):
        return ref.at[pl.ds(o, sub), :] if key_axis == 0 else ref.at[:, pl.ds(o, sub)]

    def count(pred):
        def body(c, acc):
            off = pl.multiple_of(c * chunk, chunk)
            for u in range(tiles):
                o = off + u * sub
                acc = acc + pred(tile(s_ref, o)[...], o)
            return acc
        acc = lax.fori_loop(0, nch, body, jnp.zeros(tshape, F32))
        return jnp.sum(acc, axis=key_axis, keepdims=True)

    def wide(v):
        return jnp.broadcast_to(v, tshape)

    search = n_adm > k

    def bit_cond(st):
        i, _, done = st
        return jnp.logical_and(i < 32, jnp.sum(done) < float(width))

    def bit_body(st):
        i, cur, done = st
        trial = cur | lax.shift_left(jnp.int32(1), 31 - i)
        thr_w = wide(_ukey_to_f32(trial))
        cnt = count(lambda t, o: jnp.where(t >= thr_w, 1.0, 0.0))
        cur = jnp.where(jnp.logical_and(done == 0.0, cnt >= kf), trial, cur)
        done = jnp.where(cnt == kf, 1.0, done)
        return i + 1, cur, done

    _, cur, _ = lax.while_loop(bit_cond, bit_body,
                               (jnp.int32(0), jnp.zeros(stat_shape, I32), jnp.where(search, 0.0, 1.0).astype(F32)))
    thr = jnp.where(search, _ukey_to_f32(cur), F32_LOWEST)
    thr_w = wide(thr)
    cnt_ge = count(lambda t, o: jnp.where(t >= thr_w, 1.0, 0.0))
    tie = jnp.logical_and(search, cnt_ge > kf)
    n_tie = jnp.sum(jnp.where(tie, 1.0, 0.0))

    def write(bias_fn):
        def body(c, carry):
            off = pl.multiple_of(c * chunk, chunk)
            for u in range(tiles):
                o = off + u * sub
                tile(o_ref, o)[...] = bias_fn(tile(s_ref, o)[...], o).astype(o_ref.dtype)
            return carry
        lax.fori_loop(0, nch, body, 0)

    @pl.when(n_tie == 0.0)
    def _():
        write(lambda t, o: jnp.where(t >= thr_w, 0.0, NEG_BIAS))

    @pl.when(n_tie > 0.0)
    def _():
        cnt_gt = count(lambda t, o: jnp.where(t > thr_w, 1.0, 0.0))
        need = jnp.where(search, kf - cnt_gt, float(2 * nkeys))
        tc = chunk if key_axis == 0 else LANES
        ri = lax.broadcasted_iota(I32, (tc, tc), 0)
        ci = lax.broadcasted_iota(I32, (tc, tc), 1)
        tri = jnp.where((ri >= ci) if key_axis == 0 else (ri <= ci), 1.0, 0.0).astype(BF16)

        def body(c, seen):
            off = pl.multiple_of(c * tc, tc)
            src = s_ref.at[pl.ds(off, tc), :] if key_axis == 0 else s_ref.at[:, pl.ds(off, tc)]
            dst = o_ref.at[pl.ds(off, tc), :] if key_axis == 0 else o_ref.at[:, pl.ds(off, tc)]
            t = src[...]
            eq = t == thr
            eqb = jnp.where(eq, 1.0, 0.0).astype(BF16)
            if key_axis == 0:
                rank = jnp.dot(tri, eqb, preferred_element_type=F32) + seen
                last = rank[tc - 1:tc, :]
            else:
                rank = jnp.dot(eqb, tri, preferred_element_type=F32) + seen
                last = seen + jnp.sum(jnp.where(eq, 1.0, 0.0), axis=1, keepdims=True)
            dst[...] = jnp.where(eq, jnp.where(rank <= need, 0.0, NEG_BIAS),
                                 jnp.where(t > thr, 0.0, NEG_BIAS)).astype(o_ref.dtype)
            return last

        lax.fori_loop(0, nch * (chunk // tc), body, jnp.zeros(stat_shape, F32))


def _index_topk_kernel(qpt_ref, wit_ref, kp_ref, o_ref, s_ref, *, seq, cols, schunk, chunk, k):
    j = pl.program_id(1)
    nkeys = j * cols + cols
    nch = (nkeys + chunk - 1) // chunk
    nsc = (nkeys + schunk - 1) // schunk
    w = wit_ref[...]
    qpos = j * cols + lax.broadcasted_iota(I32, (schunk, cols), 1)
    krow = lax.broadcasted_iota(I32, (schunk, cols), 0)

    def score_body(c, carry):
        off = pl.multiple_of(c * schunk, schunk)
        kc = kp_ref[pl.ds(off, schunk), :]
        acc = None
        for pr in range(IDX_HEADS // 2):
            st = jnp.dot(kc, qpt_ref[:, pr * 2 * cols:(pr + 1) * 2 * cols], preferred_element_type=F32)
            for e in range(2):
                h = 2 * pr + e
                term = w[h:h + 1, :] * jnp.maximum(st[:, e * cols:(e + 1) * cols], 0.0)
                acc = term if acc is None else acc + term
        s_ref[pl.ds(off, schunk), :] = jnp.where(krow + off <= qpos, acc, -jnp.inf)
        return carry

    lax.fori_loop(0, nsc, score_body, 0)
    n_adm = j * cols + lax.broadcasted_iota(I32, (1, cols), 1) + 1
    _select_to_bias(s_ref, o_ref, nch, n_adm, key_axis=0, width=cols, chunk=chunk, k=k, nkeys=seq)

    def fill(c, carry):
        off = pl.multiple_of(c * chunk, chunk)
        o_ref[pl.ds(off, chunk), :] = jnp.full((chunk, cols), NEG_BIAS, o_ref.dtype)
        return carry

    lax.fori_loop(nch, seq // chunk, fill, 0)


def _index_topk(qpt, wit, kp, *, batch, seq, k):
    cols = min(QUERY_BLOCK, seq)
    chunk = min(512, seq)
    schunk = min(1024, seq)
    nqb = seq // cols
    return pl.pallas_call(
        functools.partial(_index_topk_kernel, seq=seq, cols=cols, schunk=schunk, chunk=chunk, k=k),
        out_shape=jax.ShapeDtypeStruct((batch, seq, seq), BF16),
        grid=(batch, nqb),
        in_specs=[pl.BlockSpec((2 * LANES, IDX_HEADS * cols), lambda b, j: (0, b * nqb + j)),
                  pl.BlockSpec((IDX_HEADS, cols), lambda b, j: (0, b * nqb + j)),
                  pl.BlockSpec((seq, 2 * LANES), lambda b, j: (b, 0))],
        out_specs=pl.BlockSpec((None, seq, cols), lambda b, j: (b, 0, j)),
        scratch_shapes=[pltpu.VMEM((seq, cols), F32)],
        compiler_params=_cparams(("parallel", "arbitrary")),
        name="index_topk_prompt",
    )(qpt, wit, kp)


def _attn_kernel(qt_tab, kt_tab, qt_ref, kx_ref, vt_ref, b_ref, o_ref, m_ref, l_ref, acc_ref, *, tq, tk):
    p = pl.program_id(1)
    kt = kt_tab[p]
    last = (qt_tab[p] * tq + tq - 1) // tk

    @pl.when(kt == 0)
    def _():
        m_ref[...] = jnp.full(m_ref.shape, NEG_BIAS, F32)
        l_ref[...] = jnp.zeros(l_ref.shape, F32)
        acc_ref[...] = jnp.zeros(acc_ref.shape, F32)

    def over_keys(op, x):
        parts = [x[i:i + 64] for i in range(0, x.shape[0], 64)]
        while len(parts) > 1:
            parts = [op(parts[i], parts[i + 1]) for i in range(0, len(parts), 2)]
        return (jnp.max if op is jnp.maximum else jnp.sum)(parts[0], axis=0, keepdims=True)

    bias = b_ref[...].astype(F32)
    scores = [jnp.dot(kx_ref[h], qt_ref[h // 2], preferred_element_type=F32) + bias for h in range(N_HEADS)]
    for h, s in enumerate(scores):
        rows = slice(h * HEAD_DIM, (h + 1) * HEAD_DIM)
        m_prev = m_ref[h:h + 1, :]
        m_new = jnp.maximum(m_prev, over_keys(jnp.maximum, s))
        alpha = jnp.exp2(m_prev - m_new)
        pm = jnp.exp2(s - m_new)
        l_ref[h:h + 1, :] = alpha * l_ref[h:h + 1, :] + over_keys(jnp.add, pm)
        m_ref[h:h + 1, :] = m_new
        acc_ref[rows, :] = alpha * acc_ref[rows, :] + jnp.dot(vt_ref[rows, :], pm.astype(BF16),
                                                              preferred_element_type=F32)

    @pl.when(kt == last)
    def _():
        for h in range(N_HEADS):
            rows = slice(h * HEAD_DIM, (h + 1) * HEAD_DIM)
            acc_ref[rows, :] = acc_ref[rows, :] / l_ref[h:h + 1, :]
        o_ref[...] = acc_ref[...].T.astype(o_ref.dtype)


def _attention(qt4, kx, vt, bias, *, batch, seq):
    tq = min(512, seq)
    tk = min(512, seq)
    nq, nk = seq // tq, seq // tk
    qt, kt = [], []
    for i in range(nq):
        for c in range((i * tq + tq - 1) // tk + 1):
            qt.append(i)
            kt.append(c)
    npairs = len(qt)
    m = batch * seq
    return pl.pallas_call(
        functools.partial(_attn_kernel, tq=tq, tk=tk),
        out_shape=jax.ShapeDtypeStruct((m, D_ATTN), BF16),
        grid_spec=pltpu.PrefetchScalarGridSpec(
            num_scalar_prefetch=2,
            grid=(batch, npairs),
            in_specs=[
                pl.BlockSpec((N_HEADS // 2, LANES, tq), lambda b, p, qt, kt: (0, 0, b * nq + qt[p])),
                pl.BlockSpec((N_HEADS, tk, LANES), lambda b, p, qt, kt: (0, b * nk + kt[p], 0)),
                pl.BlockSpec((D_ATTN, tk), lambda b, p, qt, kt: (0, b * nk + kt[p])),
                pl.BlockSpec((None, tk, tq), lambda b, p, qt, kt: (b, kt[p], qt[p])),
            ],
            out_specs=pl.BlockSpec((tq, D_ATTN), lambda b, p, qt, kt: (b * nq + qt[p], 0)),
            scratch_shapes=[pltpu.VMEM((N_HEADS, tq), F32), pltpu.VMEM((N_HEADS, tq), F32),
                            pltpu.VMEM((D_ATTN, tq), F32)],
        ),
        compiler_params=_cparams(("parallel", "arbitrary")),
        name="sparse_attention_prompt",
    )(jnp.asarray(qt, I32), jnp.asarray(kt, I32), qt4, kx, vt, bias)


def _rglru_gates(xc, wa_ref, ba_ref, wg_ref, bg_ref, lam_ref, store):
    xcb = xc.astype(BF16)
    lam = lam_ref[...]
    nlam = -lam
    softplus = jnp.maximum(nlam, 0.0) + jnp.log1p(jnp.exp(-jnp.abs(nlam)))
    bw = xc.shape[1] // RNN_BLOCKS
    for n in range(RNN_BLOCKS):
        sl = slice(n * bw, (n + 1) * bw)
        r = _sigmoid(jnp.dot(xcb[:, sl], wa_ref[n], preferred_element_type=F32) + ba_ref[:, sl])
        ig = _sigmoid(jnp.dot(xcb[:, sl], wg_ref[n], preferred_element_type=F32) + bg_ref[:, sl])
        log_a = (-RG_C) * r * softplus[:, sl]
        a = jnp.exp(log_a)
        bx = jnp.sqrt(-_expm1(2.0 * log_a)) * (ig * xc[:, sl])
        store(sl, a, bx)


def _rglru_kernel(xr_ref, gr_ref, cw_ref, cb_ref, wa_ref, ba_ref, wg_ref, bg_ref, lam_ref,
                  y_ref, hl_ref, xbuf, a_s, b_s, h_s, *, tc):
    c = pl.program_id(1)
    pad = 8

    @pl.when(c == 0)
    def _():
        xbuf[0:pad] = jnp.zeros((pad, xbuf.shape[1]), F32)
        h_s[...] = jnp.zeros(h_s.shape, F32)

    x = xr_ref[...]
    xbuf[pad:pad + tc] = x
    cw = cw_ref[...]
    xc = cb_ref[...] + cw[CONV_W - 1:CONV_W] * x
    for jj in range(1, CONV_W):
        xc = xc + cw[CONV_W - 1 - jj:CONV_W - jj] * xbuf[pad - jj:pad - jj + tc]
    xbuf[0:pad] = x[tc - pad:tc]

    def store(sl, a, bx):
        a_s[:, sl] = a
        b_s[:, sl] = bx

    _rglru_gates(xc, wa_ref, ba_ref, wg_ref, bg_ref, lam_ref, store)

    def step(t, h):
        h = a_s[pl.ds(t, 1), :] * h + b_s[pl.ds(t, 1), :]
        b_s[pl.ds(t, 1), :] = h
        return h

    h = lax.fori_loop(0, tc, step, h_s[...], unroll=8)
    h_s[...] = h
    hl_ref[0] = h
    y_ref[...] = (b_s[...] * _gelu_tanh(gr_ref[...])).astype(y_ref.dtype)


def _rglru(xr, gr, cw, cb, wa, ba, wg, bg, lam, *, batch, seq):
    tc = min(512, seq)
    nc = seq // tc
    m, c = xr.shape
    row = pl.BlockSpec((tc, c), lambda b, i: (b * nc + i, 0))
    vec = pl.BlockSpec((1, c), lambda b, i: (0, 0))
    blk = pl.BlockSpec(wa.shape, lambda b, i: (0, 0, 0))
    return pl.pallas_call(
        functools.partial(_rglru_kernel, tc=tc),
        out_shape=[jax.ShapeDtypeStruct((m, c), BF16), jax.ShapeDtypeStruct((batch, 1, c), F32)],
        grid=(batch, nc),
        in_specs=[row, row, pl.BlockSpec((CONV_W, c), lambda b, i: (0, 0)), vec, blk, vec, blk, vec, vec],
        out_specs=[row, pl.BlockSpec((1, 1, c), lambda b, i: (b, 0, 0))],
        scratch_shapes=[pltpu.VMEM((tc + 8, c), F32), pltpu.VMEM((tc, c), F32), pltpu.VMEM((tc, c), F32),
                        pltpu.VMEM((1, c), F32)],
        compiler_params=_cparams(("parallel", "arbitrary")),
        name="rglru_prompt",
    )(xr, gr, cw, cb, wa, ba, wg, bg, lam)


def _rglru_step_kernel(xr_ref, gr_ref, sc_ref, h0_ref, cw_ref, cb_ref, wa_ref, ba_ref, wg_ref, bg_ref, lam_ref,
                       y_ref, h_ref):
    x = xr_ref[...]
    cw = cw_ref[...]
    xc = cb_ref[...] + cw[CONV_W - 1:CONV_W] * x
    for jj in range(CONV_W - 1):
        xc = xc + cw[jj:jj + 1] * sc_ref[jj]
    h0 = h0_ref[...]

    def store(sl, a, bx):
        h_ref[:, sl] = a * h0[:, sl] + bx

    _rglru_gates(xc, wa_ref, ba_ref, wg_ref, bg_ref, lam_ref, store)
    y_ref[...] = (h_ref[...] * _gelu_tanh(gr_ref[...])).astype(y_ref.dtype)


def _rglru_step(xr, gr, sc, h0, cw, cb, wa, ba, wg, bg, lam):
    m, c = xr.shape
    return pl.pallas_call(
        _rglru_step_kernel,
        out_shape=[jax.ShapeDtypeStruct((m, c), BF16), jax.ShapeDtypeStruct((m, c), F32)],
        compiler_params=pltpu.CompilerParams(vmem_limit_bytes=VMEM_LIMIT),
        name="rglru_sample",
    )(xr, gr, sc, h0, cw, cb, wa, ba, wg, bg, lam)


def _post_kernel(x_ref, o_ref, y_ref, ga_ref, gb_ref, pe_ref,
                 woa_ref, wob_ref, wout_ref, g1_ref, b1_ref, wf1_ref, wf2_ref, g2_ref, b2_ref,
                 wpe_ref, wpg_ref, bpg_ref, out_ref, *, alpha, ff_chunk):
    dot = functools.partial(jnp.dot, preferred_element_type=F32)
    merged = (_sigmoid(ga_ref[...]) * dot(o_ref[...], woa_ref[...])
              + _sigmoid(gb_ref[...]) * dot(y_ref[...], wob_ref[...]))
    x1 = _layer_norm(alpha * x_ref[...] + dot(merged.astype(BF16), wout_ref[...]), g1_ref[...], b1_ref[...])
    x1b = x1.astype(BF16)
    d_ff = wf1_ref.shape[1]
    ff = None
    for c0 in range(0, d_ff, ff_chunk):
        hcol = jnp.maximum(dot(x1b, wf1_ref[:, c0:c0 + ff_chunk]), 0.0)
        part = dot((hcol * hcol).astype(BF16), wf2_ref[c0:c0 + ff_chunk, :])
        ff = part if ff is None else ff + part
    x2 = _layer_norm(alpha * x1 + ff, g2_ref[...], b2_ref[...])
    gate = _sigmoid(dot(x2.astype(BF16), wpg_ref[...]) + bpg_ref[...])
    out_ref[...] = x2 + gate * dot(pe_ref[...].astype(BF16), wpe_ref[...])


def _post(x, o, y, ga, gb, pe, lw, *, alpha, tm, name):
    m, d = x.shape
    row = lambda n: pl.BlockSpec((tm, n), lambda i: (i, 0))
    weights = (lw["w_oa"], lw["w_ob"], lw["w_out"], lw["ln1_g"], lw["ln1_b"], lw["w_ff1"], lw["w_ff2"],
               lw["ln2_g"], lw["ln2_b"], lw["w_pe"], lw["w_pg"], lw["b_pg"])
    return pl.pallas_call(
        functools.partial(_post_kernel, alpha=alpha, ff_chunk=min(1024, lw["w_ff1"].shape[1])),
        out_shape=jax.ShapeDtypeStruct((m, d), F32),
        grid=(m // tm,),
        in_specs=[row(d), row(o.shape[1]), row(y.shape[1]), row(d), row(d), row(pe.shape[1])]
        + [_resident(w.shape) for w in weights],
        out_specs=row(d),
        compiler_params=_cparams(("parallel",)),
        name=name,
    )(x, o, y, ga, gb, pe, *weights)


def _page_specs(block, layer, n_pages, npg):
    nd = len(block)

    def index_map(b, g, pt, i):
        return (layer, pt[b * n_pages + g * npg + i]) + (0,) * (nd - 2)

    return [pl.BlockSpec(block, functools.partial(index_map, i=i)) for i in range(npg)]


def _sample_scores_kernel(pt_ref, qp_ref, w_ref, *refs, npg):
    pages, o_ref = refs[:npg], refs[npg]
    qp = qp_ref[0]
    q2 = jnp.concatenate([qp[:, 0:IDX_DIM], qp[:, 2 * IDX_DIM:3 * IDX_DIM]], axis=0)
    w = w_ref[0]
    for i, pg in enumerate(pages):
        kh, kl = _split_hi_lo(pg[...])
        ab = jnp.dot(q2, kh, preferred_element_type=F32) + jnp.dot(q2, kl, preferred_element_type=F32)
        s = ab[0:IDX_HEADS] + ab[IDX_HEADS:]
        o_ref[0, :, i * PAGE_SIZE:(i + 1) * PAGE_SIZE] = jnp.sum(w * jnp.maximum(s, 0.0), axis=0, keepdims=True)


def _sample_scores(page_table_flat, qp_s, wi_s, cache_kidx_t, layer, *, n_pages, npg):
    nb = qp_s.shape[0]
    ngrp = n_pages // npg
    return pl.pallas_call(
        functools.partial(_sample_scores_kernel, npg=npg),
        out_shape=jax.ShapeDtypeStruct((nb, 1, n_pages * PAGE_SIZE), F32),
        grid_spec=pltpu.PrefetchScalarGridSpec(
            num_scalar_prefetch=1,
            grid=(nb, ngrp),
            in_specs=[pl.BlockSpec((1, IDX_HEADS, 2 * LANES), lambda b, g, pt: (b, 0, 0)),
                      pl.BlockSpec((1, IDX_HEADS, 1), lambda b, g, pt: (b, 0, 0))]
            + _page_specs((None, None, IDX_DIM, PAGE_SIZE), layer, n_pages, npg),
            out_specs=pl.BlockSpec((1, 1, npg * PAGE_SIZE), lambda b, g, pt: (b, 0, g)),
        ),
        compiler_params=_cparams(("parallel", "arbitrary")),
        name="index_scores_sample",
    )(page_table_flat, qp_s, wi_s, *([cache_kidx_t] * npg))


def _sample_select_kernel(sp_ref, ss_ref, o_ref, s_ref, *, past, k, chunk):
    rows = sp_ref.shape[0]
    ncols = past + LANES
    s_ref[:, 0:past] = sp_ref[...]
    lane = lax.broadcasted_iota(I32, (rows, LANES), 1)
    s_ref[:, past:ncols] = jnp.where(lane == 0, ss_ref[...], -jnp.inf)
    n_adm = jnp.full((rows, 1), past + 1, I32)
    _select_to_bias(s_ref, o_ref, ncols // chunk, n_adm, key_axis=1, width=rows, chunk=chunk, k=k, nkeys=ncols)


def _sample_select(scores_past, score_self, *, k):
    nb, past = scores_past.shape
    ncols = past + LANES
    chunk = LANES * math.gcd(ncols // LANES, 5)
    return pl.pallas_call(
        functools.partial(_sample_select_kernel, past=past, k=k, chunk=chunk),
        out_shape=jax.ShapeDtypeStruct((nb, ncols), F32),
        scratch_shapes=[pltpu.VMEM((nb, ncols), F32)],
        compiler_params=pltpu.CompilerParams(vmem_limit_bytes=VMEM_LIMIT),
        name="index_select_sample",
    )(scores_past, score_self)


def _sample_attn_kernel(pt_ref, q_ref, bp_ref, bs_ref, ks_ref, vs_ref, *refs, npg):
    kpages, vpages = refs[:npg], refs[npg:2 * npg]
    o_ref, m_ref, l_ref, acc_ref = refs[2 * npg:]
    g = pl.program_id(1)

    @pl.when(g == 0)
    def _():
        m_ref[...] = jnp.full(m_ref.shape, NEG_BIAS, F32)
        l_ref[...] = jnp.zeros(l_ref.shape, F32)
        acc_ref[...] = jnp.zeros(acc_ref.shape, F32)

    head_of_lane = lax.broadcasted_iota(I32, (N_HEADS, D_ATTN), 1) // HEAD_DIM
    own = head_of_lane == lax.broadcasted_iota(I32, (N_HEADS, D_ATTN), 0)
    qrows = jnp.where(own, q_ref[0], 0.0)
    qb = qrows.astype(BF16)
    flat = lambda pg: pg[...].reshape(D_ATTN, PAGE_SIZE).astype(BF16)
    s = jnp.concatenate([jnp.dot(qb, flat(pg), preferred_element_type=F32) for pg in kpages], axis=1)
    s = s + bp_ref[0]
    m_prev = m_ref[...]
    m_new = jnp.maximum(m_prev, jnp.max(s, axis=1, keepdims=True))
    alpha = jnp.exp(m_prev - m_new)
    pm = jnp.exp(s - m_new)
    l_ref[...] = alpha * l_ref[...] + jnp.sum(pm, axis=1, keepdims=True)
    m_ref[...] = m_new
    pmb = pm.astype(BF16)
    nt = (((1,), (1,)), ((), ()))
    pv = None
    for i, pg in enumerate(vpages):
        part = lax.dot_general(pmb[:, i * PAGE_SIZE:(i + 1) * PAGE_SIZE], flat(pg), nt, preferred_element_type=F32)
        pv = part if pv is None else pv + part
    acc_ref[...] = alpha * acc_ref[...] + pv

    @pl.when(g == pl.num_programs(1) - 1)
    def _():
        rnd = lambda x: x.astype(BF16).astype(F32)
        s_self = jnp.sum(rnd(qrows) * rnd(ks_ref[0]), axis=1, keepdims=True) + bs_ref[0]
        m_prev = m_ref[...]
        m_new = jnp.maximum(m_prev, s_self)
        alpha = jnp.exp(m_prev - m_new)
        p_self = jnp.exp(s_self - m_new)
        l = alpha * l_ref[...] + p_self
        acc = alpha * acc_ref[...] + rnd(p_self) * rnd(vs_ref[0])
        o_ref[0] = jnp.sum(jnp.where(own, acc / l, 0.0), axis=0, keepdims=True).astype(o_ref.dtype)


def _sample_attention(page_table_flat, q_s, bias_past, bias_self, k_s, v_s, cache_k_t, cache_v_t, layer, *,
                      n_pages, npg):
    nb = q_s.shape[0]
    ngrp = n_pages // npg
    kv_specs = _page_specs((None, None, N_HEADS, HEAD_DIM, PAGE_SIZE), layer, n_pages, npg)
    per_seq = lambda n: pl.BlockSpec((1, 1, n), lambda b, g, pt: (b, 0, 0))
    return pl.pallas_call(
        functools.partial(_sample_attn_kernel, npg=npg),
        out_shape=jax.ShapeDtypeStruct((nb, 1, D_ATTN), BF16),
        grid_spec=pltpu.PrefetchScalarGridSpec(
            num_scalar_prefetch=1,
            grid=(nb, ngrp),
            in_specs=[per_seq(D_ATTN),
                      pl.BlockSpec((1, 1, npg * PAGE_SIZE), lambda b, g, pt: (b, 0, g)),
                      per_seq(1), per_seq(D_ATTN), per_seq(D_ATTN)] + kv_specs + kv_specs,
            out_specs=per_seq(D_ATTN),
            scratch_shapes=[pltpu.VMEM((N_HEADS, 1), F32), pltpu.VMEM((N_HEADS, 1), F32),
                            pltpu.VMEM((N_HEADS, D_ATTN), F32)],
        ),
        compiler_params=_cparams(("parallel", "arbitrary")),
        name="sparse_attention_sample",
    )(page_table_flat, q_s, bias_past, bias_self, k_s, v_s, *([cache_k_t] * npg), *([cache_v_t] * npg))


def _pack_layer(l, w_in, b_in, conv_w, conv_b, rg_wa, rg_ba, rg_wi, rg_bi, rg_lam, w_oa, w_ob, w_out,
                ln1_g, ln1_b, w_ff1, w_ff2, ln2_g, ln2_b, w_pe, w_pg, b_pg):
    d_rnn = conv_w.shape[-1]
    w, b = w_in[l], b_in[l]
    n_idx = 4 * D_ATTN
    ki0, wi0, r0 = n_idx, n_idx + IDX_DIM, n_idx + IDX_DIM + IDX_HEADS

    def pad_cols(a, n):
        return jnp.pad(a, ((0, 0), (0, n - a.shape[1])))

    b2 = b[None, :]
    w_head = jnp.concatenate([w[:, :n_idx], pad_cols(w[:, ki0:wi0], LANES), pad_cols(w[:, wi0:r0], LANES)], axis=1)
    b_head = jnp.concatenate([b2[:, :n_idx], pad_cols(b2[:, ki0:wi0], LANES), pad_cols(b2[:, wi0:r0], LANES)], axis=1)
    vec = lambda a: a[l][None, :]
    return dict(
        w_head=w_head.astype(BF16), b_head=b_head, w_rest=w[:, r0:].astype(BF16), b_rest=b2[:, r0:],
        conv_w=conv_w[l], conv_b=vec(conv_b), rg_wa=rg_wa[l].astype(BF16), rg_ba=vec(rg_ba),
        rg_wi=rg_wi[l].astype(BF16), rg_bi=vec(rg_bi), rg_lam=vec(rg_lam),
        w_oa=w_oa[l].astype(BF16), w_ob=w_ob[l].astype(BF16), w_out=w_out[l].astype(BF16),
        ln1_g=vec(ln1_g), ln1_b=vec(ln1_b), w_ff1=w_ff1[l].astype(BF16), w_ff2=w_ff2[l].astype(BF16),
        ln2_g=vec(ln2_g), ln2_b=vec(ln2_b), w_pe=w_pe[l].astype(BF16), w_pg=w_pg[l].astype(BF16), b_pg=vec(b_pg),
        d_rnn=d_rnn)


def kernel(x_prompt, x_sample, p_prompt, p_sample, cache_k, cache_v, cache_kidx, state_h, state_conv, page_table, w_in, b_in, conv_w, conv_b, rg_wa, rg_ba, rg_wi, rg_bi, rg_lam, w_oa, w_ob, w_out, ln1_g, ln1_b, w_ff1, w_ff2, ln2_g, ln2_b, w_pe, w_pg, b_pg):
    batch, seq, d_model = x_prompt.shape
    nb, dec_seq, _ = x_sample.shape
    assert dec_seq == 1
    depth = w_in.shape[0]
    d_rnn = conv_w.shape[-1]
    n_pages = page_table.shape[1]
    past = n_pages * PAGE_SIZE
    alpha = (2.0 * depth) ** 0.25
    k_prompt_sel = min(TOPK_MAX, seq // 4)
    k_sample_sel = min(TOPK_MAX, (past + dec_seq) // 4)
    m = batch * seq
    tm = min(256, m)
    npg = math.gcd(n_pages, 8)
    pt_flat = page_table.reshape(-1).astype(I32)
    cache_k_t = jnp.transpose(cache_k, (0, 1, 3, 4, 2))
    cache_v_t = jnp.transpose(cache_v, (0, 1, 3, 4, 2))
    cache_kidx_t = jnp.transpose(cache_kidx, (0, 1, 3, 2))

    xp = x_prompt.reshape(m, d_model)
    xs = x_sample.reshape(nb, d_model)
    outs = [[] for _ in range(10)]
    for l in range(depth):
        lw = _pack_layer(l, w_in, b_in, conv_w, conv_b, rg_wa, rg_ba, rg_wi, rg_bi, rg_lam, w_oa, w_ob, w_out,
                         ln1_g, ln1_b, w_ff1, w_ff2, ln2_g, ln2_b, w_pe, w_pg, b_pg)
        rg = (lw["conv_w"], lw["conv_b"], lw["rg_wa"], lw["rg_ba"], lw["rg_wi"], lw["rg_bi"], lw["rg_lam"])

        (qt4, kx, vt, k32, v32, qpt, kp, ki32, wit, xr, gr, ga, gb) = _inproj(
            xp, lw["w_head"], lw["b_head"], lw["w_rest"], lw["b_rest"], prompt=True, tm=tm)
        bias = _index_topk(qpt, wit, kp, batch=batch, seq=seq, k=k_prompt_sel)
        o_attn = _attention(qt4, kx, vt, bias, batch=batch, seq=seq)
        y_rnn, h_last = _rglru(xr, gr, *rg, batch=batch, seq=seq)
        xp = _post(xp, o_attn, y_rnn, ga, gb, p_prompt[l].reshape(m, -1), lw, alpha=alpha, tm=tm,
                   name="post_prompt")
        outs[0].append(k32.reshape(batch, seq, N_HEADS, HEAD_DIM))
        outs[1].append(v32.reshape(batch, seq, N_HEADS, HEAD_DIM))
        outs[2].append(ki32.reshape(batch, seq, IDX_DIM))
        outs[3].append(h_last.reshape(batch, d_rnn))
        outs[4].append(xr.reshape(batch, seq, d_rnn)[:, seq - (CONV_W - 1):])

        (q_s, k_s, v_s, qp_s, ki_s, wi_s, ss_s, xr_s, gr_s, ga_s, gb_s) = _inproj(
            xs, lw["w_head"], lw["b_head"], lw["w_rest"], lw["b_rest"], prompt=False, tm=nb)
        scores_past = _sample_scores(pt_flat, jnp.swapaxes(qp_s, 0, 1), wi_s.reshape(nb, IDX_HEADS, 1),
                                     cache_kidx_t, l, n_pages=n_pages, npg=npg)
        bias_s = _sample_select(scores_past.reshape(nb, past), ss_s, k=k_sample_sel)
        o_s = _sample_attention(
            pt_flat, q_s.reshape(nb, 1, D_ATTN), bias_s[:, :past].reshape(nb, 1, past),
            bias_s[:, past:past + 1].reshape(nb, 1, 1), k_s.reshape(nb, 1, D_ATTN), v_s.reshape(nb, 1, D_ATTN),
            cache_k_t, cache_v_t, l, n_pages=n_pages, npg=npg)
        y_s, h_s = _rglru_step(xr_s, gr_s, jnp.swapaxes(state_conv[l], 0, 1), state_h[l], *rg)
        xs = _post(xs, o_s.reshape(nb, D_ATTN), y_s, ga_s, gb_s, p_sample[l].reshape(nb, -1), lw, alpha=alpha,
                   tm=nb, name="post_sample")
        outs[5].append(k_s.reshape(nb, 1, N_HEADS, HEAD_DIM))
        outs[6].append(v_s.reshape(nb, 1, N_HEADS, HEAD_DIM))
        outs[7].append(ki_s.reshape(nb, 1, IDX_DIM))
        outs[8].append(h_s)
        outs[9].append(jnp.concatenate([state_conv[l][:, 1:], xr_s[:, None, :]], axis=1))

    return (xp.reshape(batch, seq, d_model), xs.reshape(nb, 1, d_model)) + tuple(jnp.stack(o) for o in outs)
```

```python
import functools
import math

import numpy as np
import jax
import jax.numpy as jnp
from jax import lax
from jax.experimental import pallas as pl
from jax.experimental.pallas import tpu as pltpu

F32 = jnp.float32
BF16 = jnp.bfloat16
I32 = jnp.int32

N_HEADS = 8
HEAD_DIM = 64
D_ATTN = N_HEADS * HEAD_DIM
IDX_HEADS = 8
IDX_DIM = 64
TOPK_MAX = 256
PAGE_SIZE = 128
RNN_BLOCKS = 8
CONV_W = 4
RG_C = 8.0
LN_EPS = 1e-5

LANES = 128
VMEM_LIMIT = 56 * 1024 * 1024
NEG_BIAS = -1e30
QUERY_BLOCK = 256
LOG2E = math.log2(math.e)
F32_LOWEST = float(np.finfo(np.float32).min)

_SEG = {}
_off = 0
for _name, _n in (("q", D_ATTN), ("k", D_ATTN), ("v", D_ATTN), ("qi", IDX_HEADS * IDX_DIM),
                  ("ki", LANES), ("wi", LANES)):
    _SEG[_name] = (_off, _off + _n)
    _off += _n


def _cparams(sem):
    return pltpu.CompilerParams(dimension_semantics=sem, vmem_limit_bytes=VMEM_LIMIT)


def _resident(shape):
    nd = len(shape)
    return pl.BlockSpec(shape, lambda *_: (0,) * nd, pipeline_mode=pl.Buffered(1))


def _sigmoid(x):
    return 1.0 / (1.0 + jnp.exp(-x))


def _gelu_tanh(x):
    c = math.sqrt(2.0 / math.pi)
    return 0.5 * x * (1.0 + jnp.tanh(c * (x + 0.044715 * (x * x * x))))


def _expm1(x):
    u = jnp.exp(x)
    um1 = u - 1.0
    return jnp.where(u == 1.0, x, jnp.where(x < -0.5, um1, um1 * x / jnp.log(u)))


def _layer_norm(x, g, b):
    mu = jnp.mean(x, axis=-1, keepdims=True)
    xc = x - mu
    var = jnp.mean(xc * xc, axis=-1, keepdims=True)
    return xc * lax.rsqrt(var + LN_EPS) * g + b


def _split_hi_lo(x):
    hi = x.astype(BF16)
    lo = (x - hi.astype(F32)).astype(BF16)
    return hi, lo


def _swap_halves(t):
    return jnp.concatenate([t[:, HEAD_DIM:], t[:, :HEAD_DIM]], axis=1)


def _inproj_kernel(x_ref, w_ref, b_ref, wr_ref, br_ref, *out_refs, prompt, d_rnn):
    if prompt:
        (qt_ref, kx_ref, vt_ref, k_ref, v_ref, qpt_ref, kp_ref, ki_ref, wit_ref,
         xr_ref, gr_ref, ga_ref, gb_ref) = out_refs
    else:
        (q_ref, k_ref, v_ref, qp_ref, ki_ref, wi_ref, ss_ref,
         xr_ref, gr_ref, ga_ref, gb_ref) = out_refs
    xb = x_ref[...].astype(BF16)
    tm = xb.shape[0]

    def seg(name):
        a, b = _SEG[name]
        return jnp.dot(xb, w_ref[:, a:b], preferred_element_type=F32) + b_ref[:, a:b]

    lane = lax.broadcasted_iota(I32, (tm, LANES), 1)
    low = lane < HEAD_DIM

    zq = seg("q") * (HEAD_DIM ** -0.5 * (LOG2E if prompt else 1.0))
    if prompt:
        for p in range(N_HEADS // 2):
            qt_ref[p] = zq[:, p * LANES:(p + 1) * LANES].T.astype(BF16)
    else:
        q_ref[...] = zq

    zk = seg("k")
    k_ref[...] = zk
    zv = seg("v")
    v_ref[...] = zv
    if prompt:
        zkb = zk.astype(BF16)
        zero_b = jnp.zeros((tm, LANES), BF16)
        for p in range(N_HEADS // 2):
            kt = zkb[:, p * LANES:(p + 1) * LANES]
            kx_ref[2 * p] = jnp.where(low, kt, zero_b)
            kx_ref[2 * p + 1] = jnp.where(low, zero_b, kt)
            vt_ref[p * LANES:(p + 1) * LANES, :] = zv[:, p * LANES:(p + 1) * LANES].T.astype(BF16)

    zqi = seg("qi") * (IDX_DIM ** -0.5)
    zki = seg("ki")
    ki_ref[...] = zki[:, :IDX_DIM]
    zwi = seg("wi") * (IDX_HEADS ** -0.5)
    kdup = jnp.where(low, zki, _swap_halves(zki))
    if prompt:
        wit_ref[...] = zwi.T[:IDX_HEADS]
        khi, klo = _split_hi_lo(kdup)
        khl = jnp.where(low, khi, klo)
        kp_ref[:, 0:LANES] = khl
        kp_ref[:, LANES:2 * LANES] = khl
    else:
        wi_ref[...] = zwi[:, :IDX_HEADS]
    qblk = min(QUERY_BLOCK, tm)
    n_qblk = tm // qblk if prompt else 0
    for p in range(IDX_HEADS // 2):
        t = zqi[:, p * LANES:(p + 1) * LANES]
        sw = _swap_halves(t)
        for e in range(2):
            h = 2 * p + e
            dup = jnp.where(low, t, sw) if e == 0 else jnp.where(low, sw, t)
            if prompt:
                hi, lo = _split_hi_lo(dup.T)
                for r in range(n_qblk):
                    c0 = (r * IDX_HEADS + h) * qblk
                    qpt_ref[0:LANES, c0:c0 + qblk] = hi[:, r * qblk:(r + 1) * qblk]
                    qpt_ref[LANES:2 * LANES, c0:c0 + qblk] = lo[:, r * qblk:(r + 1) * qblk]
            else:
                hi, lo = _split_hi_lo(dup)
                qp_ref[h, :, 0:LANES] = hi
                qp_ref[h, :, LANES:2 * LANES] = lo
    if not prompt:
        acc = jnp.zeros((tm, 1), F32)
        for p in range(IDX_HEADS // 2):
            prod = zqi[:, p * LANES:(p + 1) * LANES] * kdup
            for e in range(2):
                h = 2 * p + e
                keep = low if e == 0 else jnp.logical_not(low)
                s_h = jnp.sum(jnp.where(keep, prod, 0.0), axis=1, keepdims=True)
                acc = acc + zwi[:, h:h + 1] * jnp.maximum(s_h, 0.0)
        ss_ref[...] = acc

    for i, ref in enumerate((xr_ref, gr_ref, ga_ref, gb_ref)):
        a, b = i * d_rnn, (i + 1) * d_rnn
        ref[...] = jnp.dot(xb, wr_ref[:, a:b], preferred_element_type=F32) + br_ref[:, a:b]


def _inproj(x2d, w_head, b_head, w_rest, b_rest, *, prompt, tm):
    m, d = x2d.shape
    d_rnn = w_rest.shape[1] // 4
    sds = jax.ShapeDtypeStruct
    row = lambda n: pl.BlockSpec((tm, n), lambda i: (i, 0))
    col = lambda n: pl.BlockSpec((n, tm), lambda i: (0, i))
    wide = [sds((m, d_rnn), F32)] * 4
    wide_specs = [row(d_rnn)] * 4
    if prompt:
        qpt_cols = IDX_HEADS * tm
        out_shape = [sds((N_HEADS // 2, LANES, m), BF16), sds((N_HEADS, m, LANES), BF16),
                     sds((D_ATTN, m), BF16), sds((m, D_ATTN), F32), sds((m, D_ATTN), F32),
                     sds((2 * LANES, IDX_HEADS * m), BF16), sds((m, 2 * LANES), BF16),
                     sds((m, IDX_DIM), F32), sds((IDX_HEADS, m), F32)] + wide
        out_specs = [pl.BlockSpec((N_HEADS // 2, LANES, tm), lambda i: (0, 0, i)),
                     pl.BlockSpec((N_HEADS, tm, LANES), lambda i: (0, i, 0)),
                     col(D_ATTN), row(D_ATTN), row(D_ATTN),
                     pl.BlockSpec((2 * LANES, qpt_cols), lambda i: (0, i)), row(2 * LANES),
                     row(IDX_DIM), col(IDX_HEADS)] + wide_specs
    else:
        out_shape = [sds((m, D_ATTN), F32), sds((m, D_ATTN), F32), sds((m, D_ATTN), F32),
                     sds((IDX_HEADS, m, 2 * LANES), BF16), sds((m, IDX_DIM), F32), sds((m, IDX_HEADS), F32),
                     sds((m, 1), F32)] + wide
        out_specs = [row(D_ATTN), row(D_ATTN), row(D_ATTN),
                     pl.BlockSpec((IDX_HEADS, tm, 2 * LANES), lambda i: (0, i, 0)),
                     row(IDX_DIM), row(IDX_HEADS), row(1)] + wide_specs
    return pl.pallas_call(
        functools.partial(_inproj_kernel, prompt=prompt, d_rnn=d_rnn),
        out_shape=out_shape,
        grid=(m // tm,),
        in_specs=[row(d), _resident(w_head.shape), _resident(b_head.shape),
                  _resident(w_rest.shape), _resident(b_rest.shape)],
        out_specs=out_specs,
        compiler_params=_cparams(("parallel",)),
        name="inproj_prompt" if prompt else "inproj_sample",
    )(x2d, w_head, b_head, w_rest, b_rest)


def _ukey_to_f32(u):
    sk = u ^ jnp.int32(-2 ** 31)
    bits = jnp.where(sk >= 0, sk, sk ^ jnp.int32(2 ** 31 - 1))
    return lax.bitcast_convert_type(bits, F32)


def _select_to_bias(s_ref, o_ref, nch, n_adm, *, key_axis, width, chunk, k, nkeys, bounds=None):
    kf = float(k)
    sub = 64 if key_axis == 0 else LANES
    tiles = chunk // sub
    tshape = (sub, width) if key_axis == 0 else (width, sub)
    stat_shape = (1, width) if key_axis == 0 else (width, 1)

    def tile(ref, o):
        return ref.at[pl.ds(o, sub), :] if key_axis == 0 else ref.at[:, pl.ds(o, sub)]

    def count(pred):
        def body(c, acc):
            off = pl.multiple_of(c * chunk, chunk)
            for u in range(tiles):
                o = off + u * sub
                acc = acc + pred(tile(s_ref, o)[...], o)
            return acc
        acc = lax.fori_loop(0, nch, body, jnp.zeros(tshape, F32))
        return jnp.sum(acc, axis=key_axis, keepdims=True)

    def wide(v):
        return jnp.broadcast_to(v, tshape)

    search = n_adm > k
    done0 = jnp.where(search, 0.0, 1.0).astype(F32)

    def count_ge(v):
        v_w = wide(v)
        return count(lambda t, o: jnp.where(t >= v_w, 1.0, 0.0))

    def all_done(done):
        return jnp.sum(done) >= float(width)

    if bounds is None:
        def bit_body(st):
            i, cur, done = st
            trial = cur | lax.shift_left(jnp.int32(1), 31 - i)
            cnt = count_ge(_ukey_to_f32(trial))
            cur = jnp.where(jnp.logical_and(done == 0.0, cnt >= kf), trial, cur)
            return i + 1, cur, jnp.where(cnt == kf, 1.0, done)

        _, cur, _ = lax.while_loop(lambda st: jnp.logical_and(st[0] < 32, jnp.logical_not(all_done(st[2]))),
                                   bit_body, (jnp.int32(0), jnp.zeros(stat_shape, I32), done0))
        kth = _ukey_to_f32(cur)
    else:
        lo, hi = bounds
        hi = hi + 0.0
        hb = lax.bitcast_convert_type(hi, I32)
        hi = lax.bitcast_convert_type(jnp.where(hi >= 0.0, hb + 1, hb - 1), F32)
        zero = jnp.zeros(stat_shape, F32)
        n_ge0 = count_ge(zero)
        n_gt0 = count(lambda t, o: jnp.where(t > 0.0, 1.0, 0.0))
        at_zero = jnp.logical_and(n_ge0 >= kf, n_gt0 < kf)
        lo = jnp.where(n_ge0 >= kf, jnp.maximum(lo, 0.0), lo)
        hi = jnp.where(n_ge0 >= kf, hi, jnp.minimum(hi, 0.0))
        done = jnp.where(at_zero, 1.0, done0)
        kth0 = jnp.where(at_zero, 0.0, lo)

        def mid_of(lo, hi):
            mid = 0.5 * lo + 0.5 * hi
            return mid, jnp.logical_or(mid <= lo, mid >= hi)

        def bis_body(st):
            i, lo, hi, kth, done = st
            mid, stuck = mid_of(lo, hi)
            cnt = count_ge(mid)
            open_ = jnp.logical_and(done == 0.0, jnp.logical_not(stuck))
            kth = jnp.where(done == 0.0, jnp.where(stuck, lo, jnp.where(cnt >= kf, mid, kth)), kth)
            done = jnp.where(jnp.logical_or(stuck, cnt == kf), 1.0, done)
            ge = cnt >= kf
            lo = jnp.where(jnp.logical_and(open_, ge), mid, lo)
            hi = jnp.where(jnp.logical_and(open_, jnp.logical_not(ge)), mid, hi)
            return i + 1, lo, hi, kth, done

        st = lax.while_loop(lambda st: jnp.logical_and(st[0] < 320, jnp.logical_not(all_done(st[4]))),
                            bis_body, (jnp.int32(0), lo, hi, kth0, done))
        kth = st[3]
    thr = jnp.where(search, kth, F32_LOWEST)
    thr_w = wide(thr)
    cnt_ge = count(lambda t, o: jnp.where(t >= thr_w, 1.0, 0.0))
    tie = jnp.logical_and(search, cnt_ge > kf)
    n_tie = jnp.sum(jnp.where(tie, 1.0, 0.0))

    def write(bias_fn):
        def body(c, carry):
            off = pl.multiple_of(c * chunk, chunk)
            for u in range(tiles):
                o = off + u * sub
                tile(o_ref, o)[...] = bias_fn(tile(s_ref, o)[...], o).astype(o_ref.dtype)
            return carry
        lax.fori_loop(0, nch, body, 0)

    @pl.when(n_tie == 0.0)
    def _():
        write(lambda t, o: jnp.where(t >= thr_w, 0.0, NEG_BIAS))

    @pl.when(n_tie > 0.0)
    def _():
        cnt_gt = count(lambda t, o: jnp.where(t > thr_w, 1.0, 0.0))
        need = jnp.where(search, kf - cnt_gt, float(2 * nkeys))
        tc = chunk if key_axis == 0 else LANES
        ri = lax.broadcasted_iota(I32, (tc, tc), 0)
        ci = lax.broadcasted_iota(I32, (tc, tc), 1)
        tri = jnp.where((ri >= ci) if key_axis == 0 else (ri <= ci), 1.0, 0.0).astype(BF16)

        def body(c, seen):
            off = pl.multiple_of(c * tc, tc)
            src = s_ref.at[pl.ds(off, tc), :] if key_axis == 0 else s_ref.at[:, pl.ds(off, tc)]
            dst = o_ref.at[pl.ds(off, tc), :] if key_axis == 0 else o_ref.at[:, pl.ds(off, tc)]
            t = src[...]
            eq = t == thr
            eqb = jnp.where(eq, 1.0, 0.0).astype(BF16)
            if key_axis == 0:
                rank = jnp.dot(tri, eqb, preferred_element_type=F32) + seen
                last = rank[tc - 1:tc, :]
            else:
                rank = jnp.dot(eqb, tri, preferred_element_type=F32) + seen
                last = seen + jnp.sum(jnp.where(eq, 1.0, 0.0), axis=1, keepdims=True)
            dst[...] = jnp.where(eq, jnp.where(rank <= need, 0.0, NEG_BIAS),
                                 jnp.where(t > thr, 0.0, NEG_BIAS)).astype(o_ref.dtype)
            return last

        lax.fori_loop(0, nch * (chunk // tc), body, jnp.zeros(stat_shape, F32))


def _index_topk_kernel(qpt_ref, wit_ref, kp_ref, o_ref, s_ref, *, seq, cols, schunk, chunk, k):
    j = pl.program_id(1)
    nkeys = j * cols + cols
    nch = (nkeys + chunk - 1) // chunk
    nsc = (nkeys + schunk - 1) // schunk
    w = wit_ref[...]
    qpos = j * cols + lax.broadcasted_iota(I32, (schunk, cols), 1)
    krow = lax.broadcasted_iota(I32, (schunk, cols), 0)

    def score_body(c, carry):
        off = pl.multiple_of(c * schunk, schunk)
        kc = kp_ref[pl.ds(off, schunk), :]
        acc = None
        for pr in range(IDX_HEADS // 2):
            st = jnp.dot(kc, qpt_ref[:, pr * 2 * cols:(pr + 1) * 2 * cols], preferred_element_type=F32)
            for e in range(2):
                h = 2 * pr + e
                term = w[h:h + 1, :] * jnp.maximum(st[:, e * cols:(e + 1) * cols], 0.0)
                acc = term if acc is None else acc + term
        s_ref[pl.ds(off, schunk), :] = jnp.where(krow + off <= qpos, acc, -jnp.inf)
        return carry

    lax.fori_loop(0, nsc, score_body, 0)
    n_adm = j * cols + lax.broadcasted_iota(I32, (1, cols), 1) + 1
    bounds = None
    if cols >= k:
        def fold(c, mx):
            return jnp.maximum(mx, s_ref[pl.ds(pl.multiple_of(c * cols, cols), cols), :])
        class_max = lax.fori_loop(0, nkeys // cols, fold, jnp.full((cols, cols), -jnp.inf, F32))
        bounds = (jnp.min(class_max, axis=0, keepdims=True), jnp.max(class_max, axis=0, keepdims=True))
    _select_to_bias(s_ref, o_ref, nch, n_adm, key_axis=0, width=cols, chunk=chunk, k=k, nkeys=seq, bounds=bounds)

    def fill(c, carry):
        off = pl.multiple_of(c * chunk, chunk)
        o_ref[pl.ds(off, chunk), :] = jnp.full((chunk, cols), NEG_BIAS, o_ref.dtype)
        return carry

    lax.fori_loop(nch, seq // chunk, fill, 0)


def _index_topk(qpt, wit, kp, *, batch, seq, k):
    cols = min(QUERY_BLOCK, seq)
    chunk = min(512, seq)
    schunk = min(1024, seq)
    nqb = seq // cols
    return pl.pallas_call(
        functools.partial(_index_topk_kernel, seq=seq, cols=cols, schunk=schunk, chunk=chunk, k=k),
        out_shape=jax.ShapeDtypeStruct((batch, seq, seq), BF16),
        grid=(batch, nqb),
        in_specs=[pl.BlockSpec((2 * LANES, IDX_HEADS * cols), lambda b, j: (0, b * nqb + j)),
                  pl.BlockSpec((IDX_HEADS, cols), lambda b, j: (0, b * nqb + j)),
                  pl.BlockSpec((seq, 2 * LANES), lambda b, j: (b, 0))],
        out_specs=pl.BlockSpec((None, seq, cols), lambda b, j: (b, 0, j)),
        scratch_shapes=[pltpu.VMEM((seq, cols), F32)],
        compiler_params=_cparams(("parallel", "arbitrary")),
        name="index_topk_prompt",
    )(qpt, wit, kp)


def _attn_kernel(qt_tab, kt_tab, qt_ref, kx_ref, vt_ref, b_ref, o_ref, m_ref, l_ref, acc_ref, *, tq, tk):
    p = pl.program_id(1)
    kt = kt_tab[p]
    last = (qt_tab[p] * tq + tq - 1) // tk

    @pl.when(kt == 0)
    def _():
        m_ref[...] = jnp.full(m_ref.shape, NEG_BIAS, F32)
        l_ref[...] = jnp.zeros(l_ref.shape, F32)
        acc_ref[...] = jnp.zeros(acc_ref.shape, F32)

    def over_keys(op, x):
        parts = [x[i:i + 64] for i in range(0, x.shape[0], 64)]
        while len(parts) > 1:
            parts = [op(parts[i], parts[i + 1]) for i in range(0, len(parts), 2)]
        return (jnp.max if op is jnp.maximum else jnp.sum)(parts[0], axis=0, keepdims=True)

    bias = b_ref[...].astype(F32)
    scores = [jnp.dot(kx_ref[h], qt_ref[h // 2], preferred_element_type=F32) + bias for h in range(N_HEADS)]
    for h, s in enumerate(scores):
        rows = slice(h * HEAD_DIM, (h + 1) * HEAD_DIM)
        m_prev = m_ref[h:h + 1, :]
        m_new = jnp.maximum(m_prev, over_keys(jnp.maximum, s))
        alpha = jnp.exp2(m_prev - m_new)
        pm = jnp.exp2(s - m_new)
        l_ref[h:h + 1, :] = alpha * l_ref[h:h + 1, :] + over_keys(jnp.add, pm)
        m_ref[h:h + 1, :] = m_new
        acc_ref[rows, :] = alpha * acc_ref[rows, :] + jnp.dot(vt_ref[rows, :], pm.astype(BF16),
                                                              preferred_element_type=F32)

    @pl.when(kt == last)
    def _():
        for h in range(N_HEADS):
            rows = slice(h * HEAD_DIM, (h + 1) * HEAD_DIM)
            acc_ref[rows, :] = acc_ref[rows, :] / l_ref[h:h + 1, :]
        o_ref[...] = acc_ref[...].T.astype(o_ref.dtype)


def _attention(qt4, kx, vt, bias, *, batch, seq):
    tq = min(512, seq)
    tk = min(512, seq)
    nq, nk = seq // tq, seq // tk
    qt, kt = [], []
    for i in range(nq):
        for c in range((i * tq + tq - 1) // tk + 1):
            qt.append(i)
            kt.append(c)
    npairs = len(qt)
    m = batch * seq
    return pl.pallas_call(
        functools.partial(_attn_kernel, tq=tq, tk=tk),
        out_shape=jax.ShapeDtypeStruct((m, D_ATTN), BF16),
        grid_spec=pltpu.PrefetchScalarGridSpec(
            num_scalar_prefetch=2,
            grid=(batch, npairs),
            in_specs=[
                pl.BlockSpec((N_HEADS // 2, LANES, tq), lambda b, p, qt, kt: (0, 0, b * nq + qt[p])),
                pl.BlockSpec((N_HEADS, tk, LANES), lambda b, p, qt, kt: (0, b * nk + kt[p], 0)),
                pl.BlockSpec((D_ATTN, tk), lambda b, p, qt, kt: (0, b * nk + kt[p])),
                pl.BlockSpec((None, tk, tq), lambda b, p, qt, kt: (b, kt[p], qt[p])),
            ],
            out_specs=pl.BlockSpec((tq, D_ATTN), lambda b, p, qt, kt: (b * nq + qt[p], 0)),
            scratch_shapes=[pltpu.VMEM((N_HEADS, tq), F32), pltpu.VMEM((N_HEADS, tq), F32),
                            pltpu.VMEM((D_ATTN, tq), F32)],
        ),
        compiler_params=_cparams(("parallel", "arbitrary")),
        name="sparse_attention_prompt",
    )(jnp.asarray(qt, I32), jnp.asarray(kt, I32), qt4, kx, vt, bias)


def _rglru_gates(xc, wa_ref, ba_ref, wg_ref, bg_ref, lam_ref, store):
    xcb = xc.astype(BF16)
    lam = lam_ref[...]
    nlam = -lam
    softplus = jnp.maximum(nlam, 0.0) + jnp.log1p(jnp.exp(-jnp.abs(nlam)))
    bw = xc.shape[1] // RNN_BLOCKS
    for n in range(RNN_BLOCKS):
        sl = slice(n * bw, (n + 1) * bw)
        r = _sigmoid(jnp.dot(xcb[:, sl], wa_ref[n], preferred_element_type=F32) + ba_ref[:, sl])
        ig = _sigmoid(jnp.dot(xcb[:, sl], wg_ref[n], preferred_element_type=F32) + bg_ref[:, sl])
        log_a = (-RG_C) * r * softplus[:, sl]
        a = jnp.exp(log_a)
        bx = jnp.sqrt(-_expm1(2.0 * log_a)) * (ig * xc[:, sl])
        store(sl, a, bx)


def _rglru_kernel(xr_ref, gr_ref, cw_ref, cb_ref, wa_ref, ba_ref, wg_ref, bg_ref, lam_ref,
                  y_ref, hl_ref, xbuf, a_s, b_s, h_s, *, tc):
    c = pl.program_id(1)
    pad = 8

    @pl.when(c == 0)
    def _():
        xbuf[0:pad] = jnp.zeros((pad, xbuf.shape[1]), F32)
        h_s[...] = jnp.zeros(h_s.shape, F32)

    x = xr_ref[...]
    xbuf[pad:pad + tc] = x
    cw = cw_ref[...]
    xc = cb_ref[...] + cw[CONV_W - 1:CONV_W] * x
    for jj in range(1, CONV_W):
        xc = xc + cw[CONV_W - 1 - jj:CONV_W - jj] * xbuf[pad - jj:pad - jj + tc]
    xbuf[0:pad] = x[tc - pad:tc]

    def store(sl, a, bx):
        a_s[:, sl] = a
        b_s[:, sl] = bx

    _rglru_gates(xc, wa_ref, ba_ref, wg_ref, bg_ref, lam_ref, store)

    def step(t, h):
        h = a_s[pl.ds(t, 1), :] * h + b_s[pl.ds(t, 1), :]
        b_s[pl.ds(t, 1), :] = h
        return h

    h = lax.fori_loop(0, tc, step, h_s[...], unroll=8)
    h_s[...] = h
    hl_ref[0] = h
    y_ref[...] = (b_s[...] * _gelu_tanh(gr_ref[...])).astype(y_ref.dtype)


def _rglru(xr, gr, cw, cb, wa, ba, wg, bg, lam, *, batch, seq):
    tc = min(512, seq)
    nc = seq // tc
    m, c = xr.shape
    row = pl.BlockSpec((tc, c), lambda b, i: (b * nc + i, 0))
    vec = pl.BlockSpec((1, c), lambda b, i: (0, 0))
    blk = pl.BlockSpec(wa.shape, lambda b, i: (0, 0, 0))
    return pl.pallas_call(
        functools.partial(_rglru_kernel, tc=tc),
        out_shape=[jax.ShapeDtypeStruct((m, c), BF16), jax.ShapeDtypeStruct((batch, 1, c), F32)],
        grid=(batch, nc),
        in_specs=[row, row, pl.BlockSpec((CONV_W, c), lambda b, i: (0, 0)), vec, blk, vec, blk, vec, vec],
        out_specs=[row, pl.BlockSpec((1, 1, c), lambda b, i: (b, 0, 0))],
        scratch_shapes=[pltpu.VMEM((tc + 8, c), F32), pltpu.VMEM((tc, c), F32), pltpu.VMEM((tc, c), F32),
                        pltpu.VMEM((1, c), F32)],
        compiler_params=_cparams(("parallel", "arbitrary")),
        name="rglru_prompt",
    )(xr, gr, cw, cb, wa, ba, wg, bg, lam)


def _rglru_step_kernel(xr_ref, gr_ref, sc_ref, h0_ref, cw_ref, cb_ref, wa_ref, ba_ref, wg_ref, bg_ref, lam_ref,
                       y_ref, h_ref):
    x = xr_ref[...]
    cw = cw_ref[...]
    xc = cb_ref[...] + cw[CONV_W - 1:CONV_W] * x
    for jj in range(CONV_W - 1):
        xc = xc + cw[jj:jj + 1] * sc_ref[jj]
    h0 = h0_ref[...]

    def store(sl, a, bx):
        h_ref[:, sl] = a * h0[:, sl] + bx

    _rglru_gates(xc, wa_ref, ba_ref, wg_ref, bg_ref, lam_ref, store)
    y_ref[...] = (h_ref[...] * _gelu_tanh(gr_ref[...])).astype(y_ref.dtype)


def _rglru_step(xr, gr, sc, h0, cw, cb, wa, ba, wg, bg, lam):
    m, c = xr.shape
    return pl.pallas_call(
        _rglru_step_kernel,
        out_shape=[jax.ShapeDtypeStruct((m, c), BF16), jax.ShapeDtypeStruct((m, c), F32)],
        compiler_params=pltpu.CompilerParams(vmem_limit_bytes=VMEM_LIMIT),
        name="rglru_sample",
    )(xr, gr, sc, h0, cw, cb, wa, ba, wg, bg, lam)


def _post_kernel(x_ref, o_ref, y_ref, ga_ref, gb_ref, pe_ref,
                 woa_ref, wob_ref, wout_ref, g1_ref, b1_ref, wf1_ref, wf2_ref, g2_ref, b2_ref,
                 wpe_ref, wpg_ref, bpg_ref, out_ref, *, alpha, ff_chunk):
    dot = functools.partial(jnp.dot, preferred_element_type=F32)
    merged = (_sigmoid(ga_ref[...]) * dot(o_ref[...], woa_ref[...])
              + _sigmoid(gb_ref[...]) * dot(y_ref[...], wob_ref[...]))
    x1 = _layer_norm(alpha * x_ref[...] + dot(merged.astype(BF16), wout_ref[...]), g1_ref[...], b1_ref[...])
    x1b = x1.astype(BF16)
    d_ff = wf1_ref.shape[1]
    ff = None
    for c0 in range(0, d_ff, ff_chunk):
        hcol = jnp.maximum(dot(x1b, wf1_ref[:, c0:c0 + ff_chunk]), 0.0)
        part = dot((hcol * hcol).astype(BF16), wf2_ref[c0:c0 + ff_chunk, :])
        ff = part if ff is None else ff + part
    x2 = _layer_norm(alpha * x1 + ff, g2_ref[...], b2_ref[...])
    gate = _sigmoid(dot(x2.astype(BF16), wpg_ref[...]) + bpg_ref[...])
    out_ref[...] = x2 + gate * dot(pe_ref[...].astype(BF16), wpe_ref[...])


def _post(x, o, y, ga, gb, pe, lw, *, alpha, tm, name):
    m, d = x.shape
    row = lambda n: pl.BlockSpec((tm, n), lambda i: (i, 0))
    weights = (lw["w_oa"], lw["w_ob"], lw["w_out"], lw["ln1_g"], lw["ln1_b"], lw["w_ff1"], lw["w_ff2"],
               lw["ln2_g"], lw["ln2_b"], lw["w_pe"], lw["w_pg"], lw["b_pg"])
    return pl.pallas_call(
        functools.partial(_post_kernel, alpha=alpha, ff_chunk=min(1024, lw["w_ff1"].shape[1])),
        out_shape=jax.ShapeDtypeStruct((m, d), F32),
        grid=(m // tm,),
        in_specs=[row(d), row(o.shape[1]), row(y.shape[1]), row(d), row(d), row(pe.shape[1])]
        + [_resident(w.shape) for w in weights],
        out_specs=row(d),
        compiler_params=_cparams(("parallel",)),
        name=name,
    )(x, o, y, ga, gb, pe, *weights)


def _page_specs(block, layer, n_pages, npg):
    nd = len(block)

    def index_map(b, g, pt, i):
        return (layer, pt[b * n_pages + g * npg + i]) + (0,) * (nd - 2)

    return [pl.BlockSpec(block, functools.partial(index_map, i=i)) for i in range(npg)]


def _sample_scores_kernel(pt_ref, qp_ref, w_ref, *refs, npg):
    pages, o_ref = refs[:npg], refs[npg]
    qp = qp_ref[0]
    q2 = jnp.concatenate([qp[:, 0:IDX_DIM], qp[:, 2 * IDX_DIM:3 * IDX_DIM]], axis=0)
    w = w_ref[0]
    for i, pg in enumerate(pages):
        kh, kl = _split_hi_lo(pg[...])
        ab = jnp.dot(q2, kh, preferred_element_type=F32) + jnp.dot(q2, kl, preferred_element_type=F32)
        s = ab[0:IDX_HEADS] + ab[IDX_HEADS:]
        o_ref[0, :, i * PAGE_SIZE:(i + 1) * PAGE_SIZE] = jnp.sum(w * jnp.maximum(s, 0.0), axis=0, keepdims=True)


def _sample_scores(page_table_flat, qp_s, wi_s, cache_kidx_t, layer, *, n_pages, npg):
    nb = qp_s.shape[0]
    ngrp = n_pages // npg
    return pl.pallas_call(
        functools.partial(_sample_scores_kernel, npg=npg),
        out_shape=jax.ShapeDtypeStruct((nb, 1, n_pages * PAGE_SIZE), F32),
        grid_spec=pltpu.PrefetchScalarGridSpec(
            num_scalar_prefetch=1,
            grid=(nb, ngrp),
            in_specs=[pl.BlockSpec((1, IDX_HEADS, 2 * LANES), lambda b, g, pt: (b, 0, 0)),
                      pl.BlockSpec((1, IDX_HEADS, 1), lambda b, g, pt: (b, 0, 0))]
            + _page_specs((None, None, IDX_DIM, PAGE_SIZE), layer, n_pages, npg),
            out_specs=pl.BlockSpec((1, 1, npg * PAGE_SIZE), lambda b, g, pt: (b, 0, g)),
        ),
        compiler_params=_cparams(("parallel", "arbitrary")),
        name="index_scores_sample",
    )(page_table_flat, qp_s, wi_s, *([cache_kidx_t] * npg))


def _sample_select_kernel(sp_ref, ss_ref, o_ref, s_ref, *, past, k, chunk):
    rows = sp_ref.shape[0]
    ncols = past + LANES
    s_ref[:, 0:past] = sp_ref[...]
    lane = lax.broadcasted_iota(I32, (rows, LANES), 1)
    s_ref[:, past:ncols] = jnp.where(lane == 0, ss_ref[...], -jnp.inf)
    n_adm = jnp.full((rows, 1), past + 1, I32)
    _select_to_bias(s_ref, o_ref, ncols // chunk, n_adm, key_axis=1, width=rows, chunk=chunk, k=k, nkeys=ncols)


def _sample_select(scores_past, score_self, *, k):
    nb, past = scores_past.shape
    ncols = past + LANES
    chunk = LANES * math.gcd(ncols // LANES, 5)
    return pl.pallas_call(
        functools.partial(_sample_select_kernel, past=past, k=k, chunk=chunk),
        out_shape=jax.ShapeDtypeStruct((nb, ncols), F32),
        scratch_shapes=[pltpu.VMEM((nb, ncols), F32)],
        compiler_params=pltpu.CompilerParams(vmem_limit_bytes=VMEM_LIMIT),
        name="index_select_sample",
    )(scores_past, score_self)


def _sample_attn_kernel(pt_ref, q_ref, bp_ref, bs_ref, ks_ref, vs_ref, *refs, npg):
    kpages, vpages = refs[:npg], refs[npg:2 * npg]
    o_ref, m_ref, l_ref, acc_ref = refs[2 * npg:]
    g = pl.program_id(1)

    @pl.when(g == 0)
    def _():
        m_ref[...] = jnp.full(m_ref.shape, NEG_BIAS, F32)
        l_ref[...] = jnp.zeros(l_ref.shape, F32)
        acc_ref[...] = jnp.zeros(acc_ref.shape, F32)

    head_of_lane = lax.broadcasted_iota(I32, (N_HEADS, D_ATTN), 1) // HEAD_DIM
    own = head_of_lane == lax.broadcasted_iota(I32, (N_HEADS, D_ATTN), 0)
    qrows = jnp.where(own, q_ref[0], 0.0)
    qb = qrows.astype(BF16)
    flat = lambda pg: pg[...].reshape(D_ATTN, PAGE_SIZE).astype(BF16)
    s = jnp.concatenate([jnp.dot(qb, flat(pg), preferred_element_type=F32) for pg in kpages], axis=1)
    s = s + bp_ref[0]
    m_prev = m_ref[...]
    m_new = jnp.maximum(m_prev, jnp.max(s, axis=1, keepdims=True))
    alpha = jnp.exp(m_prev - m_new)
    pm = jnp.exp(s - m_new)
    l_ref[...] = alpha * l_ref[...] + jnp.sum(pm, axis=1, keepdims=True)
    m_ref[...] = m_new
    pmb = pm.astype(BF16)
    nt = (((1,), (1,)), ((), ()))
    pv = None
    for i, pg in enumerate(vpages):
        part = lax.dot_general(pmb[:, i * PAGE_SIZE:(i + 1) * PAGE_SIZE], flat(pg), nt, preferred_element_type=F32)
        pv = part if pv is None else pv + part
    acc_ref[...] = alpha * acc_ref[...] + pv

    @pl.when(g == pl.num_programs(1) - 1)
    def _():
        rnd = lambda x: x.astype(BF16).astype(F32)
        s_self = jnp.sum(rnd(qrows) * rnd(ks_ref[0]), axis=1, keepdims=True) + bs_ref[0]
        m_prev = m_ref[...]
        m_new = jnp.maximum(m_prev, s_self)
        alpha = jnp.exp(m_prev - m_new)
        p_self = jnp.exp(s_self - m_new)
        l = alpha * l_ref[...] + p_self
        acc = alpha * acc_ref[...] + rnd(p_self) * rnd(vs_ref[0])
        o_ref[0] = jnp.sum(jnp.where(own, acc / l, 0.0), axis=0, keepdims=True).astype(o_ref.dtype)


def _sample_attention(page_table_flat, q_s, bias_past, bias_self, k_s, v_s, cache_k_t, cache_v_t, layer, *,
                      n_pages, npg):
    nb = q_s.shape[0]
    ngrp = n_pages // npg
    kv_specs = _page_specs((None, None, N_HEADS, HEAD_DIM, PAGE_SIZE), layer, n_pages, npg)
    per_seq = lambda n: pl.BlockSpec((1, 1, n), lambda b, g, pt: (b, 0, 0))
    return pl.pallas_call(
        functools.partial(_sample_attn_kernel, npg=npg),
        out_shape=jax.ShapeDtypeStruct((nb, 1, D_ATTN), BF16),
        grid_spec=pltpu.PrefetchScalarGridSpec(
            num_scalar_prefetch=1,
            grid=(nb, ngrp),
            in_specs=[per_seq(D_ATTN),
                      pl.BlockSpec((1, 1, npg * PAGE_SIZE), lambda b, g, pt: (b, 0, g)),
                      per_seq(1), per_seq(D_ATTN), per_seq(D_ATTN)] + kv_specs + kv_specs,
            out_specs=per_seq(D_ATTN),
            scratch_shapes=[pltpu.VMEM((N_HEADS, 1), F32), pltpu.VMEM((N_HEADS, 1), F32),
                            pltpu.VMEM((N_HEADS, D_ATTN), F32)],
        ),
        compiler_params=_cparams(("parallel", "arbitrary")),
        name="sparse_attention_sample",
    )(page_table_flat, q_s, bias_past, bias_self, k_s, v_s, *([cache_k_t] * npg), *([cache_v_t] * npg))


def _pack_layer(l, w_in, b_in, conv_w, conv_b, rg_wa, rg_ba, rg_wi, rg_bi, rg_lam, w_oa, w_ob, w_out,
                ln1_g, ln1_b, w_ff1, w_ff2, ln2_g, ln2_b, w_pe, w_pg, b_pg):
    d_rnn = conv_w.shape[-1]
    w, b = w_in[l], b_in[l]
    n_idx = 4 * D_ATTN
    ki0, wi0, r0 = n_idx, n_idx + IDX_DIM, n_idx + IDX_DIM + IDX_HEADS

    def pad_cols(a, n):
        return jnp.pad(a, ((0, 0), (0, n - a.shape[1])))

    b2 = b[None, :]
    w_head = jnp.concatenate([w[:, :n_idx], pad_cols(w[:, ki0:wi0], LANES), pad_cols(w[:, wi0:r0], LANES)], axis=1)
    b_head = jnp.concatenate([b2[:, :n_idx], pad_cols(b2[:, ki0:wi0], LANES), pad_cols(b2[:, wi0:r0], LANES)], axis=1)
    vec = lambda a: a[l][None, :]
    return dict(
        w_head=w_head.astype(BF16), b_head=b_head, w_rest=w[:, r0:].astype(BF16), b_rest=b2[:, r0:],
        conv_w=conv_w[l], conv_b=vec(conv_b), rg_wa=rg_wa[l].astype(BF16), rg_ba=vec(rg_ba),
        rg_wi=rg_wi[l].astype(BF16), rg_bi=vec(rg_bi), rg_lam=vec(rg_lam),
        w_oa=w_oa[l].astype(BF16), w_ob=w_ob[l].astype(BF16), w_out=w_out[l].astype(BF16),
        ln1_g=vec(ln1_g), ln1_b=vec(ln1_b), w_ff1=w_ff1[l].astype(BF16), w_ff2=w_ff2[l].astype(BF16),
        ln2_g=vec(ln2_g), ln2_b=vec(ln2_b), w_pe=w_pe[l].astype(BF16), w_pg=w_pg[l].astype(BF16), b_pg=vec(b_pg),
        d_rnn=d_rnn)


def kernel(x_prompt, x_sample, p_prompt, p_sample, cache_k, cache_v, cache_kidx, state_h, state_conv, page_table, w_in, b_in, conv_w, conv_b, rg_wa, rg_ba, rg_wi, rg_bi, rg_lam, w_oa, w_ob, w_out, ln1_g, ln1_b, w_ff1, w_ff2, ln2_g, ln2_b, w_pe, w_pg, b_pg):
    batch, seq, d_model = x_prompt.shape
    nb, dec_seq, _ = x_sample.shape
    assert dec_seq == 1
    depth = w_in.shape[0]
    d_rnn = conv_w.shape[-1]
    n_pages = page_table.shape[1]
    past = n_pages * PAGE_SIZE
    alpha = (2.0 * depth) ** 0.25
    k_prompt_sel = min(TOPK_MAX, seq // 4)
    k_sample_sel = min(TOPK_MAX, (past + dec_seq) // 4)
    m = batch * seq
    tm = min(256, m)
    npg = math.gcd(n_pages, 8)
    npg_idx = math.gcd(n_pages, 32)
    pt_flat = page_table.reshape(-1).astype(I32)
    cache_k_t = jnp.transpose(cache_k, (0, 1, 3, 4, 2))
    cache_v_t = jnp.transpose(cache_v, (0, 1, 3, 4, 2))
    cache_kidx_t = jnp.transpose(cache_kidx, (0, 1, 3, 2))

    xp = x_prompt.reshape(m, d_model)
    xs = x_sample.reshape(nb, d_model)
    outs = [[] for _ in range(10)]
    for l in range(depth):
        lw = _pack_layer(l, w_in, b_in, conv_w, conv_b, rg_wa, rg_ba, rg_wi, rg_bi, rg_lam, w_oa, w_ob, w_out,
                         ln1_g, ln1_b, w_ff1, w_ff2, ln2_g, ln2_b, w_pe, w_pg, b_pg)
        rg = (lw["conv_w"], lw["conv_b"], lw["rg_wa"], lw["rg_ba"], lw["rg_wi"], lw["rg_bi"], lw["rg_lam"])

        (qt4, kx, vt, k32, v32, qpt, kp, ki32, wit, xr, gr, ga, gb) = _inproj(
            xp, lw["w_head"], lw["b_head"], lw["w_rest"], lw["b_rest"], prompt=True, tm=tm)
        bias = _index_topk(qpt, wit, kp, batch=batch, seq=seq, k=k_prompt_sel)
        o_attn = _attention(qt4, kx, vt, bias, batch=batch, seq=seq)
        y_rnn, h_last = _rglru(xr, gr, *rg, batch=batch, seq=seq)
        xp = _post(xp, o_attn, y_rnn, ga, gb, p_prompt[l].reshape(m, -1), lw, alpha=alpha, tm=tm,
                   name="post_prompt")
        outs[0].append(k32.reshape(batch, seq, N_HEADS, HEAD_DIM))
        outs[1].append(v32.reshape(batch, seq, N_HEADS, HEAD_DIM))
        outs[2].append(ki32.reshape(batch, seq, IDX_DIM))
        outs[3].append(h_last.reshape(batch, d_rnn))
        outs[4].append(xr.reshape(batch, seq, d_rnn)[:, seq - (CONV_W - 1):])

        (q_s, k_s, v_s, qp_s, ki_s, wi_s, ss_s, xr_s, gr_s, ga_s, gb_s) = _inproj(
            xs, lw["w_head"], lw["b_head"], lw["w_rest"], lw["b_rest"], prompt=False, tm=nb)
        scores_past = _sample_scores(pt_flat, jnp.swapaxes(qp_s, 0, 1), wi_s.reshape(nb, IDX_HEADS, 1),
                                     cache_kidx_t, l, n_pages=n_pages, npg=npg_idx)
        bias_s = _sample_select(scores_past.reshape(nb, past), ss_s, k=k_sample_sel)
        o_s = _sample_attention(
            pt_flat, q_s.reshape(nb, 1, D_ATTN), bias_s[:, :past].reshape(nb, 1, past),
            bias_s[:, past:past + 1].reshape(nb, 1, 1), k_s.reshape(nb, 1, D_ATTN), v_s.reshape(nb, 1, D_ATTN),
            cache_k_t, cache_v_t, l, n_pages=n_pages, npg=npg)
        y_s, h_s = _rglru_step(xr_s, gr_s, jnp.swapaxes(state_conv[l], 0, 1), state_h[l], *rg)
        xs = _post(xs, o_s.reshape(nb, D_ATTN), y_s, ga_s, gb_s, p_sample[l].reshape(nb, -1), lw, alpha=alpha,
                   tm=nb, name="post_sample")
        outs[5].append(k_s.reshape(nb, 1, N_HEADS, HEAD_DIM))
        outs[6].append(v_s.reshape(nb, 1, N_HEADS, HEAD_DIM))
        outs[7].append(ki_s.reshape(nb, 1, IDX_DIM))
        outs[8].append(h_s)
        outs[9].append(jnp.concatenate([state_conv[l][:, 1:], xr_s[:, None, :]], axis=1))

    return (xp.reshape(batch, seq, d_model), xs.reshape(nb, 1, d_model)) + tuple(jnp.stack(o) for o in outs)
```

```python
import functools
import math

import numpy as np
import jax
import jax.numpy as jnp
from jax import lax
from jax.experimental import pallas as pl
from jax.experimental.pallas import tpu as pltpu

F32 = jnp.float32
BF16 = jnp.bfloat16
I32 = jnp.int32

N_HEADS = 8
HEAD_DIM = 64
D_ATTN = N_HEADS * HEAD_DIM
IDX_HEADS = 8
IDX_DIM = 64
TOPK_MAX = 256
PAGE_SIZE = 128
RNN_BLOCKS = 8
CONV_W = 4
RG_C = 8.0
LN_EPS = 1e-5

LANES = 128
VMEM_LIMIT = 56 * 1024 * 1024
NEG_BIAS = -1e30
QUERY_BLOCK = 256
LOG2E = math.log2(math.e)
F32_LOWEST = float(np.finfo(np.float32).min)

_SEG = {}
_off = 0
for _name, _n in (("q", D_ATTN), ("k", D_ATTN), ("v", D_ATTN), ("qi", IDX_HEADS * IDX_DIM),
                  ("ki", LANES), ("wi", LANES)):
    _SEG[_name] = (_off, _off + _n)
    _off += _n


def _cparams(sem):
    return pltpu.CompilerParams(dimension_semantics=sem, vmem_limit_bytes=VMEM_LIMIT)


def _resident(shape):
    nd = len(shape)
    return pl.BlockSpec(shape, lambda *_: (0,) * nd, pipeline_mode=pl.Buffered(1))


def _sigmoid(x):
    return 1.0 / (1.0 + jnp.exp(-x))


def _gelu_tanh(x):
    c = math.sqrt(2.0 / math.pi)
    return 0.5 * x * (1.0 + jnp.tanh(c * (x + 0.044715 * (x * x * x))))


def _expm1(x):
    u = jnp.exp(x)
    um1 = u - 1.0
    return jnp.where(u == 1.0, x, jnp.where(x < -0.5, um1, um1 * x / jnp.log(u)))


def _layer_norm(x, g, b):
    mu = jnp.mean(x, axis=-1, keepdims=True)
    xc = x - mu
    var = jnp.mean(xc * xc, axis=-1, keepdims=True)
    return xc * lax.rsqrt(var + LN_EPS) * g + b


def _split_hi_lo(x):
    hi = x.astype(BF16)
    lo = (x - hi.astype(F32)).astype(BF16)
    return hi, lo


def _swap_halves(t):
    return jnp.concatenate([t[:, HEAD_DIM:], t[:, :HEAD_DIM]], axis=1)


def _inproj_kernel(x_ref, w_ref, b_ref, wr_ref, br_ref, *out_refs, prompt, d_rnn):
    if prompt:
        (qt_ref, kx_ref, vt_ref, k_ref, v_ref, qpt_ref, kp_ref, ki_ref, wit_ref,
         xr_ref, gr_ref, ga_ref, gb_ref) = out_refs
    else:
        (q_ref, k_ref, v_ref, qp_ref, ki_ref, wi_ref, ss_ref,
         xr_ref, gr_ref, ga_ref, gb_ref) = out_refs
    xb = x_ref[...].astype(BF16)
    tm = xb.shape[0]

    def seg(name):
        a, b = _SEG[name]
        return jnp.dot(xb, w_ref[:, a:b], preferred_element_type=F32) + b_ref[:, a:b]

    lane = lax.broadcasted_iota(I32, (tm, LANES), 1)
    low = lane < HEAD_DIM

    zq = seg("q") * (HEAD_DIM ** -0.5 * (LOG2E if prompt else 1.0))
    if prompt:
        for p in range(N_HEADS // 2):
            qt_ref[p] = zq[:, p * LANES:(p + 1) * LANES].T.astype(BF16)
    else:
        q_ref[...] = zq

    zk = seg("k")
    k_ref[...] = zk
    zv = seg("v")
    v_ref[...] = zv
    if prompt:
        zkb = zk.astype(BF16)
        zero_b = jnp.zeros((tm, LANES), BF16)
        for p in range(N_HEADS // 2):
            kt = zkb[:, p * LANES:(p + 1) * LANES]
            kx_ref[2 * p] = jnp.where(low, kt, zero_b)
            kx_ref[2 * p + 1] = jnp.where(low, zero_b, kt)
            vt_ref[p * LANES:(p + 1) * LANES, :] = zv[:, p * LANES:(p + 1) * LANES].T.astype(BF16)

    zqi = seg("qi") * (IDX_DIM ** -0.5)
    zki = seg("ki")
    ki_ref[...] = zki[:, :IDX_DIM]
    zwi = seg("wi") * (IDX_HEADS ** -0.5)
    kdup = jnp.where(low, zki, _swap_halves(zki))
    if prompt:
        wit_ref[...] = zwi.T[:IDX_HEADS]
        khi, klo = _split_hi_lo(kdup)
        khl = jnp.where(low, khi, klo)
        kp_ref[:, 0:LANES] = khl
        kp_ref[:, LANES:2 * LANES] = khl
    else:
        wi_ref[...] = zwi[:, :IDX_HEADS]
    qblk = min(QUERY_BLOCK, tm)
    n_qblk = tm // qblk if prompt else 0
    for p in range(IDX_HEADS // 2):
        t = zqi[:, p * LANES:(p + 1) * LANES]
        sw = _swap_halves(t)
        for e in range(2):
            h = 2 * p + e
            dup = jnp.where(low, t, sw) if e == 0 else jnp.where(low, sw, t)
            if prompt:
                hi, lo = _split_hi_lo(dup.T)
                for r in range(n_qblk):
                    c0 = (r * IDX_HEADS + h) * qblk
                    qpt_ref[0:LANES, c0:c0 + qblk] = hi[:, r * qblk:(r + 1) * qblk]
                    qpt_ref[LANES:2 * LANES, c0:c0 + qblk] = lo[:, r * qblk:(r + 1) * qblk]
            else:
                hi, lo = _split_hi_lo(dup)
                qp_ref[h, :, 0:LANES] = hi
                qp_ref[h, :, LANES:2 * LANES] = lo
    if not prompt:
        acc = jnp.zeros((tm, 1), F32)
        for p in range(IDX_HEADS // 2):
            prod = zqi[:, p * LANES:(p + 1) * LANES] * kdup
            for e in range(2):
                h = 2 * p + e
                keep = low if e == 0 else jnp.logical_not(low)
                s_h = jnp.sum(jnp.where(keep, prod, 0.0), axis=1, keepdims=True)
                acc = acc + zwi[:, h:h + 1] * jnp.maximum(s_h, 0.0)
        ss_ref[...] = acc

    for i, ref in enumerate((xr_ref, gr_ref, ga_ref, gb_ref)):
        a, b = i * d_rnn, (i + 1) * d_rnn
        ref[...] = jnp.dot(xb, wr_ref[:, a:b], preferred_element_type=F32) + br_ref[:, a:b]


def _inproj(x2d, w_head, b_head, w_rest, b_rest, *, prompt, tm):
    m, d = x2d.shape
    d_rnn = w_rest.shape[1] // 4
    sds = jax.ShapeDtypeStruct
    row = lambda n: pl.BlockSpec((tm, n), lambda i: (i, 0))
    col = lambda n: pl.BlockSpec((n, tm), lambda i: (0, i))
    wide = [sds((m, d_rnn), F32)] * 4
    wide_specs = [row(d_rnn)] * 4
    if prompt:
        qpt_cols = IDX_HEADS * tm
        out_shape = [sds((N_HEADS // 2, LANES, m), BF16), sds((N_HEADS, m, LANES), BF16),
                     sds((D_ATTN, m), BF16), sds((m, D_ATTN), F32), sds((m, D_ATTN), F32),
                     sds((2 * LANES, IDX_HEADS * m), BF16), sds((m, 2 * LANES), BF16),
                     sds((m, IDX_DIM), F32), sds((IDX_HEADS, m), F32)] + wide
        out_specs = [pl.BlockSpec((N_HEADS // 2, LANES, tm), lambda i: (0, 0, i)),
                     pl.BlockSpec((N_HEADS, tm, LANES), lambda i: (0, i, 0)),
                     col(D_ATTN), row(D_ATTN), row(D_ATTN),
                     pl.BlockSpec((2 * LANES, qpt_cols), lambda i: (0, i)), row(2 * LANES),
                     row(IDX_DIM), col(IDX_HEADS)] + wide_specs
    else:
        out_shape = [sds((m, D_ATTN), F32), sds((m, D_ATTN), F32), sds((m, D_ATTN), F32),
                     sds((IDX_HEADS, m, 2 * LANES), BF16), sds((m, IDX_DIM), F32), sds((m, IDX_HEADS), F32),
                     sds((m, 1), F32)] + wide
        out_specs = [row(D_ATTN), row(D_ATTN), row(D_ATTN),
                     pl.BlockSpec((IDX_HEADS, tm, 2 * LANES), lambda i: (0, i, 0)),
                     row(IDX_DIM), row(IDX_HEADS), row(1)] + wide_specs
    return pl.pallas_call(
        functools.partial(_inproj_kernel, prompt=prompt, d_rnn=d_rnn),
        out_shape=out_shape,
        grid=(m // tm,),
        in_specs=[row(d), _resident(w_head.shape), _resident(b_head.shape),
                  _resident(w_rest.shape), _resident(b_rest.shape)],
        out_specs=out_specs,
        compiler_params=_cparams(("parallel",)),
        name="inproj_prompt" if prompt else "inproj_sample",
    )(x2d, w_head, b_head, w_rest, b_rest)


def _ukey_to_f32(u):
    sk = u ^ jnp.int32(-2 ** 31)
    bits = jnp.where(sk >= 0, sk, sk ^ jnp.int32(2 ** 31 - 1))
    return lax.bitcast_convert_type(bits, F32)


def _select_to_bias(s_ref, o_ref, nch, n_adm, *, key_axis, width, chunk, k, nkeys, bounds=None):
    kf = float(k)
    sub = 64 if key_axis == 0 else LANES
    tiles = chunk // sub
    tshape = (sub, width) if key_axis == 0 else (width, sub)
    stat_shape = (1, width) if key_axis == 0 else (width, 1)

    def tile(ref, o):
        return ref.at[pl.ds(o, sub), :] if key_axis == 0 else ref.at[:, pl.ds(o, sub)]

    def count(pred):
        def body(c, acc):
            off = pl.multiple_of(c * chunk, chunk)
            for u in range(tiles):
                o = off + u * sub
                acc = acc + pred(tile(s_ref, o)[...], o)
            return acc
        acc = lax.fori_loop(0, nch, body, jnp.zeros(tshape, F32))
        return jnp.sum(acc, axis=key_axis, keepdims=True)

    def wide(v):
        return jnp.broadcast_to(v, tshape)

    search = n_adm > k
    done0 = jnp.where(search, 0.0, 1.0).astype(F32)

    def count_ge(v):
        v_w = wide(v)
        return count(lambda t, o: jnp.where(t >= v_w, 1.0, 0.0))

    def all_done(done):
        return jnp.sum(done) >= float(width)

    if bounds is None:
        def bit_body(st):
            i, cur, done = st
            trial = cur | lax.shift_left(jnp.int32(1), 31 - i)
            cnt = count_ge(_ukey_to_f32(trial))
            cur = jnp.where(jnp.logical_and(done == 0.0, cnt >= kf), trial, cur)
            return i + 1, cur, jnp.where(cnt == kf, 1.0, done)

        _, cur, _ = lax.while_loop(lambda st: jnp.logical_and(st[0] < 32, jnp.logical_not(all_done(st[2]))),
                                   bit_body, (jnp.int32(0), jnp.zeros(stat_shape, I32), done0))
        kth = _ukey_to_f32(cur)
    else:
        lo, hi = bounds
        hi = hi + 0.0
        hb = lax.bitcast_convert_type(hi, I32)
        hi = lax.bitcast_convert_type(jnp.where(hi >= 0.0, hb + 1, hb - 1), F32)
        zero = jnp.zeros(stat_shape, F32)
        n_ge0 = count_ge(zero)
        n_gt0 = count(lambda t, o: jnp.where(t > 0.0, 1.0, 0.0))
        at_zero = jnp.logical_and(n_ge0 >= kf, n_gt0 < kf)
        lo = jnp.where(n_ge0 >= kf, jnp.maximum(lo, 0.0), lo)
        hi = jnp.where(n_ge0 >= kf, hi, jnp.minimum(hi, 0.0))
        done = jnp.where(at_zero, 1.0, done0)
        kth0 = jnp.where(at_zero, 0.0, lo)
        ckth0 = jnp.where(at_zero, n_ge0, jnp.inf)

        def bis_body(st):
            i, lo, hi, kth, ckth, done = st
            mid = 0.5 * lo + 0.5 * hi
            stuck = jnp.logical_or(mid <= lo, mid >= hi)
            cnt = count_ge(mid)
            open_ = jnp.logical_and(done == 0.0, jnp.logical_not(stuck))
            up = jnp.logical_and(open_, cnt >= kf)
            down = jnp.logical_and(open_, cnt < kf)
            done = jnp.where(jnp.logical_or(stuck, cnt == kf), 1.0, done)
            return (i + 1, jnp.where(up, mid, lo), jnp.where(down, mid, hi), jnp.where(up, mid, kth),
                    jnp.where(up, cnt, ckth), done)

        st = lax.fori_loop(0, 16, lambda _, st: bis_body(st), (jnp.int32(0), lo, hi, kth0, ckth0, done))
        st = lax.while_loop(lambda st: jnp.logical_and(st[0] < 320, jnp.logical_not(all_done(st[5]))), bis_body, st)
        kth, cnt_ge = st[3], st[4]
    thr = jnp.where(search, kth, F32_LOWEST)
    thr_w = wide(thr)
    if bounds is None:
        cnt_ge = count_ge(thr)
    tie = jnp.logical_and(search, cnt_ge > kf)
    n_tie = jnp.sum(jnp.where(tie, 1.0, 0.0))

    def write(bias_fn):
        def body(c, carry):
            off = pl.multiple_of(c * chunk, chunk)
            for u in range(tiles):
                o = off + u * sub
                tile(o_ref, o)[...] = bias_fn(tile(s_ref, o)[...], o).astype(o_ref.dtype)
            return carry
        lax.fori_loop(0, nch, body, 0)

    @pl.when(n_tie == 0.0)
    def _():
        write(lambda t, o: jnp.where(t >= thr_w, 0.0, NEG_BIAS))

    @pl.when(n_tie > 0.0)
    def _():
        cnt_gt = count(lambda t, o: jnp.where(t > thr_w, 1.0, 0.0))
        need = jnp.where(search, kf - cnt_gt, float(2 * nkeys))
        tc = chunk if key_axis == 0 else LANES
        ri = lax.broadcasted_iota(I32, (tc, tc), 0)
        ci = lax.broadcasted_iota(I32, (tc, tc), 1)
        tri = jnp.where((ri >= ci) if key_axis == 0 else (ri <= ci), 1.0, 0.0).astype(BF16)

        def body(c, seen):
            off = pl.multiple_of(c * tc, tc)
            src = s_ref.at[pl.ds(off, tc), :] if key_axis == 0 else s_ref.at[:, pl.ds(off, tc)]
            dst = o_ref.at[pl.ds(off, tc), :] if key_axis == 0 else o_ref.at[:, pl.ds(off, tc)]
            t = src[...]
            eq = t == thr
            eqb = jnp.where(eq, 1.0, 0.0).astype(BF16)
            if key_axis == 0:
                rank = jnp.dot(tri, eqb, preferred_element_type=F32) + seen
                last = rank[tc - 1:tc, :]
            else:
                rank = jnp.dot(eqb, tri, preferred_element_type=F32) + seen
                last = seen + jnp.sum(jnp.where(eq, 1.0, 0.0), axis=1, keepdims=True)
            dst[...] = jnp.where(eq, jnp.where(rank <= need, 0.0, NEG_BIAS),
                                 jnp.where(t > thr, 0.0, NEG_BIAS)).astype(o_ref.dtype)
            return last

        lax.fori_loop(0, nch * (chunk // tc), body, jnp.zeros(stat_shape, F32))


def _index_topk_kernel(qpt_ref, wit_ref, kp_ref, o_ref, s_ref, *, seq, cols, schunk, chunk, k):
    j = pl.program_id(1)
    nkeys = j * cols + cols
    nch = (nkeys + chunk - 1) // chunk
    nsc = (nkeys + schunk - 1) // schunk
    w = wit_ref[...]
    qpos = j * cols + lax.broadcasted_iota(I32, (schunk, cols), 1)
    krow = lax.broadcasted_iota(I32, (schunk, cols), 0)

    def score_body(c, carry):
        off = pl.multiple_of(c * schunk, schunk)
        kc = kp_ref[pl.ds(off, schunk), :]
        acc = None
        for pr in range(IDX_HEADS // 2):
            st = jnp.dot(kc, qpt_ref[:, pr * 2 * cols:(pr + 1) * 2 * cols], preferred_element_type=F32)
            for e in range(2):
                h = 2 * pr + e
                term = w[h:h + 1, :] * jnp.maximum(st[:, e * cols:(e + 1) * cols], 0.0)
                acc = term if acc is None else acc + term
        s_ref[pl.ds(off, schunk), :] = jnp.where(krow + off <= qpos, acc, -jnp.inf)
        return carry

    lax.fori_loop(0, nsc, score_body, 0)
    n_adm = j * cols + lax.broadcasted_iota(I32, (1, cols), 1) + 1
    bounds = None
    if cols >= k:
        def fold(c, mx):
            return jnp.maximum(mx, s_ref[pl.ds(pl.multiple_of(c * cols, cols), cols), :])
        class_max = lax.fori_loop(0, nkeys // cols, fold, jnp.full((cols, cols), -jnp.inf, F32))
        bounds = (jnp.min(class_max, axis=0, keepdims=True), jnp.max(class_max, axis=0, keepdims=True))
    _select_to_bias(s_ref, o_ref, nch, n_adm, key_axis=0, width=cols, chunk=chunk, k=k, nkeys=seq, bounds=bounds)

    def fill(c, carry):
        off = pl.multiple_of(c * chunk, chunk)
        o_ref[pl.ds(off, chunk), :] = jnp.full((chunk, cols), NEG_BIAS, o_ref.dtype)
        return carry

    lax.fori_loop(nch, seq // chunk, fill, 0)


def _index_topk(qpt, wit, kp, *, batch, seq, k):
    cols = min(QUERY_BLOCK, seq)
    chunk = min(512, seq)
    schunk = min(1024, seq)
    nqb = seq // cols
    return pl.pallas_call(
        functools.partial(_index_topk_kernel, seq=seq, cols=cols, schunk=schunk, chunk=chunk, k=k),
        out_shape=jax.ShapeDtypeStruct((batch, seq, seq), BF16),
        grid=(batch, nqb),
        in_specs=[pl.BlockSpec((2 * LANES, IDX_HEADS * cols), lambda b, j: (0, b * nqb + j)),
                  pl.BlockSpec((IDX_HEADS, cols), lambda b, j: (0, b * nqb + j)),
                  pl.BlockSpec((seq, 2 * LANES), lambda b, j: (b, 0))],
        out_specs=pl.BlockSpec((None, seq, cols), lambda b, j: (b, 0, j)),
        scratch_shapes=[pltpu.VMEM((seq, cols), F32)],
        compiler_params=_cparams(("parallel", "arbitrary")),
        name="index_topk_prompt",
    )(qpt, wit, kp)


def _attn_kernel(qt_tab, kt_tab, qt_ref, kx_ref, vt_ref, b_ref, o_ref, m_ref, l_ref, acc_ref, *, tq, tk):
    p = pl.program_id(1)
    kt = kt_tab[p]
    last = (qt_tab[p] * tq + tq - 1) // tk

    @pl.when(kt == 0)
    def _():
        m_ref[...] = jnp.full(m_ref.shape, NEG_BIAS, F32)
        l_ref[...] = jnp.zeros(l_ref.shape, F32)
        acc_ref[...] = jnp.zeros(acc_ref.shape, F32)

    def over_keys(op, x):
        parts = [x[i:i + 64] for i in range(0, x.shape[0], 64)]
        while len(parts) > 1:
            parts = [op(parts[i], parts[i + 1]) for i in range(0, len(parts), 2)]
        return (jnp.max if op is jnp.maximum else jnp.sum)(parts[0], axis=0, keepdims=True)

    bias = b_ref[...].astype(F32)
    scores = [jnp.dot(kx_ref[h], qt_ref[h // 2], preferred_element_type=F32) + bias for h in range(N_HEADS)]
    for h, s in enumerate(scores):
        rows = slice(h * HEAD_DIM, (h + 1) * HEAD_DIM)
        m_prev = m_ref[h:h + 1, :]
        m_new = jnp.maximum(m_prev, over_keys(jnp.maximum, s))
        alpha = jnp.exp2(m_prev - m_new)
        pm = jnp.exp2(s - m_new)
        l_ref[h:h + 1, :] = alpha * l_ref[h:h + 1, :] + over_keys(jnp.add, pm)
        m_ref[h:h + 1, :] = m_new
        acc_ref[rows, :] = alpha * acc_ref[rows, :] + jnp.dot(vt_ref[rows, :], pm.astype(BF16),
                                                              preferred_element_type=F32)

    @pl.when(kt == last)
    def _():
        for h in range(N_HEADS):
            rows = slice(h * HEAD_DIM, (h + 1) * HEAD_DIM)
            acc_ref[rows, :] = acc_ref[rows, :] / l_ref[h:h + 1, :]
        o_ref[...] = acc_ref[...].T.astype(o_ref.dtype)


def _attention(qt4, kx, vt, bias, *, batch, seq):
    tq = min(512, seq)
    tk = min(512, seq)
    nq, nk = seq // tq, seq // tk
    qt, kt = [], []
    for i in range(nq):
        for c in range((i * tq + tq - 1) // tk + 1):
            qt.append(i)
            kt.append(c)
    npairs = len(qt)
    m = batch * seq
    return pl.pallas_call(
        functools.partial(_attn_kernel, tq=tq, tk=tk),
        out_shape=jax.ShapeDtypeStruct((m, D_ATTN), BF16),
        grid_spec=pltpu.PrefetchScalarGridSpec(
            num_scalar_prefetch=2,
            grid=(batch, npairs),
            in_specs=[
                pl.BlockSpec((N_HEADS // 2, LANES, tq), lambda b, p, qt, kt: (0, 0, b * nq + qt[p])),
                pl.BlockSpec((N_HEADS, tk, LANES), lambda b, p, qt, kt: (0, b * nk + kt[p], 0)),
                pl.BlockSpec((D_ATTN, tk), lambda b, p, qt, kt: (0, b * nk + kt[p])),
                pl.BlockSpec((None, tk, tq), lambda b, p, qt, kt: (b, kt[p], qt[p])),
            ],
            out_specs=pl.BlockSpec((tq, D_ATTN), lambda b, p, qt, kt: (b * nq + qt[p], 0)),
            scratch_shapes=[pltpu.VMEM((N_HEADS, tq), F32), pltpu.VMEM((N_HEADS, tq), F32),
                            pltpu.VMEM((D_ATTN, tq), F32)],
        ),
        compiler_params=_cparams(("parallel", "arbitrary")),
        name="sparse_attention_prompt",
    )(jnp.asarray(qt, I32), jnp.asarray(kt, I32), qt4, kx, vt, bias)


def _rglru_gates(xc, wa_ref, ba_ref, wg_ref, bg_ref, lam_ref, store):
    xcb = xc.astype(BF16)
    lam = lam_ref[...]
    nlam = -lam
    softplus = jnp.maximum(nlam, 0.0) + jnp.log1p(jnp.exp(-jnp.abs(nlam)))
    bw = xc.shape[1] // RNN_BLOCKS
    for n in range(RNN_BLOCKS):
        sl = slice(n * bw, (n + 1) * bw)
        r = _sigmoid(jnp.dot(xcb[:, sl], wa_ref[n], preferred_element_type=F32) + ba_ref[:, sl])
        ig = _sigmoid(jnp.dot(xcb[:, sl], wg_ref[n], preferred_element_type=F32) + bg_ref[:, sl])
        log_a = (-RG_C) * r * softplus[:, sl]
        a = jnp.exp(log_a)
        bx = jnp.sqrt(-_expm1(2.0 * log_a)) * (ig * xc[:, sl])
        store(sl, a, bx)


def _rglru_kernel(xr_ref, gr_ref, cw_ref, cb_ref, wa_ref, ba_ref, wg_ref, bg_ref, lam_ref,
                  y_ref, hl_ref, xbuf, a_s, b_s, h_s, *, tc):
    c = pl.program_id(1)
    pad = 8

    @pl.when(c == 0)
    def _():
        xbuf[0:pad] = jnp.zeros((pad, xbuf.shape[1]), F32)
        h_s[...] = jnp.zeros(h_s.shape, F32)

    x = xr_ref[...]
    xbuf[pad:pad + tc] = x
    cw = cw_ref[...]
    xc = cb_ref[...] + cw[CONV_W - 1:CONV_W] * x
    for jj in range(1, CONV_W):
        xc = xc + cw[CONV_W - 1 - jj:CONV_W - jj] * xbuf[pad - jj:pad - jj + tc]
    xbuf[0:pad] = x[tc - pad:tc]

    def store(sl, a, bx):
        a_s[:, sl] = a
        b_s[:, sl] = bx

    _rglru_gates(xc, wa_ref, ba_ref, wg_ref, bg_ref, lam_ref, store)

    def step(t, h):
        h = a_s[pl.ds(t, 1), :] * h + b_s[pl.ds(t, 1), :]
        b_s[pl.ds(t, 1), :] = h
        return h

    h = lax.fori_loop(0, tc, step, h_s[...], unroll=8)
    h_s[...] = h
    hl_ref[0] = h
    y_ref[...] = (b_s[...] * _gelu_tanh(gr_ref[...])).astype(y_ref.dtype)


def _rglru(xr, gr, cw, cb, wa, ba, wg, bg, lam, *, batch, seq):
    tc = min(512, seq)
    nc = seq // tc
    m, c = xr.shape
    row = pl.BlockSpec((tc, c), lambda b, i: (b * nc + i, 0))
    vec = pl.BlockSpec((1, c), lambda b, i: (0, 0))
    blk = pl.BlockSpec(wa.shape, lambda b, i: (0, 0, 0))
    return pl.pallas_call(
        functools.partial(_rglru_kernel, tc=tc),
        out_shape=[jax.ShapeDtypeStruct((m, c), BF16), jax.ShapeDtypeStruct((batch, 1, c), F32)],
        grid=(batch, nc),
        in_specs=[row, row, pl.BlockSpec((CONV_W, c), lambda b, i: (0, 0)), vec, blk, vec, blk, vec, vec],
        out_specs=[row, pl.BlockSpec((1, 1, c), lambda b, i: (b, 0, 0))],
        scratch_shapes=[pltpu.VMEM((tc + 8, c), F32), pltpu.VMEM((tc, c), F32), pltpu.VMEM((tc, c), F32),
                        pltpu.VMEM((1, c), F32)],
        compiler_params=_cparams(("parallel", "arbitrary")),
        name="rglru_prompt",
    )(xr, gr, cw, cb, wa, ba, wg, bg, lam)


def _rglru_step_kernel(xr_ref, gr_ref, sc_ref, h0_ref, cw_ref, cb_ref, wa_ref, ba_ref, wg_ref, bg_ref, lam_ref,
                       y_ref, h_ref):
    x = xr_ref[...]
    cw = cw_ref[...]
    xc = cb_ref[...] + cw[CONV_W - 1:CONV_W] * x
    for jj in range(CONV_W - 1):
        xc = xc + cw[jj:jj + 1] * sc_ref[jj]
    h0 = h0_ref[...]

    def store(sl, a, bx):
        h_ref[:, sl] = a * h0[:, sl] + bx

    _rglru_gates(xc, wa_ref, ba_ref, wg_ref, bg_ref, lam_ref, store)
    y_ref[...] = (h_ref[...] * _gelu_tanh(gr_ref[...])).astype(y_ref.dtype)


def _rglru_step(xr, gr, sc, h0, cw, cb, wa, ba, wg, bg, lam):
    m, c = xr.shape
    return pl.pallas_call(
        _rglru_step_kernel,
        out_shape=[jax.ShapeDtypeStruct((m, c), BF16), jax.ShapeDtypeStruct((m, c), F32)],
        compiler_params=pltpu.CompilerParams(vmem_limit_bytes=VMEM_LIMIT),
        name="rglru_sample",
    )(xr, gr, sc, h0, cw, cb, wa, ba, wg, bg, lam)


def _post_kernel(x_ref, o_ref, y_ref, ga_ref, gb_ref, pe_ref,
                 woa_ref, wob_ref, wout_ref, g1_ref, b1_ref, wf1_ref, wf2_ref, g2_ref, b2_ref,
                 wpe_ref, wpg_ref, bpg_ref, out_ref, *, alpha, ff_chunk):
    dot = functools.partial(jnp.dot, preferred_element_type=F32)
    merged = (_sigmoid(ga_ref[...]) * dot(o_ref[...], woa_ref[...])
              + _sigmoid(gb_ref[...]) * dot(y_ref[...], wob_ref[...]))
    x1 = _layer_norm(alpha * x_ref[...] + dot(merged.astype(BF16), wout_ref[...]), g1_ref[...], b1_ref[...])
    x1b = x1.astype(BF16)
    d_ff = wf1_ref.shape[1]
    ff = None
    for c0 in range(0, d_ff, ff_chunk):
        hcol = jnp.maximum(dot(x1b, wf1_ref[:, c0:c0 + ff_chunk]), 0.0)
        part = dot((hcol * hcol).astype(BF16), wf2_ref[c0:c0 + ff_chunk, :])
        ff = part if ff is None else ff + part
    x2 = _layer_norm(alpha * x1 + ff, g2_ref[...], b2_ref[...])
    gate = _sigmoid(dot(x2.astype(BF16), wpg_ref[...]) + bpg_ref[...])
    out_ref[...] = x2 + gate * dot(pe_ref[...].astype(BF16), wpe_ref[...])


def _post(x, o, y, ga, gb, pe, lw, *, alpha, tm, name):
    m, d = x.shape
    row = lambda n: pl.BlockSpec((tm, n), lambda i: (i, 0))
    weights = (lw["w_oa"], lw["w_ob"], lw["w_out"], lw["ln1_g"], lw["ln1_b"], lw["w_ff1"], lw["w_ff2"],
               lw["ln2_g"], lw["ln2_b"], lw["w_pe"], lw["w_pg"], lw["b_pg"])
    return pl.pallas_call(
        functools.partial(_post_kernel, alpha=alpha, ff_chunk=min(1024, lw["w_ff1"].shape[1])),
        out_shape=jax.ShapeDtypeStruct((m, d), F32),
        grid=(m // tm,),
        in_specs=[row(d), row(o.shape[1]), row(y.shape[1]), row(d), row(d), row(pe.shape[1])]
        + [_resident(w.shape) for w in weights],
        out_specs=row(d),
        compiler_params=_cparams(("parallel",)),
        name=name,
    )(x, o, y, ga, gb, pe, *weights)


def _page_specs(block, layer, n_pages, npg):
    nd = len(block)

    def index_map(b, g, pt, i):
        return (layer, pt[b * n_pages + g * npg + i]) + (0,) * (nd - 2)

    return [pl.BlockSpec(block, functools.partial(index_map, i=i)) for i in range(npg)]


def _sample_scores_kernel(pt_ref, qp_ref, w_ref, *refs, npg):
    pages, o_ref = refs[:npg], refs[npg]
    qp = qp_ref[0]
    q2 = jnp.concatenate([qp[:, 0:IDX_DIM], qp[:, 2 * IDX_DIM:3 * IDX_DIM]], axis=0)
    w = w_ref[0]
    for i, pg in enumerate(pages):
        kh, kl = _split_hi_lo(pg[...])
        ab = jnp.dot(q2, kh, preferred_element_type=F32) + jnp.dot(q2, kl, preferred_element_type=F32)
        s = ab[0:IDX_HEADS] + ab[IDX_HEADS:]
        o_ref[0, :, i * PAGE_SIZE:(i + 1) * PAGE_SIZE] = jnp.sum(w * jnp.maximum(s, 0.0), axis=0, keepdims=True)


def _sample_scores(page_table_flat, qp_s, wi_s, cache_kidx_t, layer, *, n_pages, npg):
    nb = qp_s.shape[0]
    ngrp = n_pages // npg
    return pl.pallas_call(
        functools.partial(_sample_scores_kernel, npg=npg),
        out_shape=jax.ShapeDtypeStruct((nb, 1, n_pages * PAGE_SIZE), F32),
        grid_spec=pltpu.PrefetchScalarGridSpec(
            num_scalar_prefetch=1,
            grid=(nb, ngrp),
            in_specs=[pl.BlockSpec((1, IDX_HEADS, 2 * LANES), lambda b, g, pt: (b, 0, 0)),
                      pl.BlockSpec((1, IDX_HEADS, 1), lambda b, g, pt: (b, 0, 0))]
            + _page_specs((None, None, IDX_DIM, PAGE_SIZE), layer, n_pages, npg),
            out_specs=pl.BlockSpec((1, 1, npg * PAGE_SIZE), lambda b, g, pt: (b, 0, g)),
        ),
        compiler_params=_cparams(("parallel", "arbitrary")),
        name="index_scores_sample",
    )(page_table_flat, qp_s, wi_s, *([cache_kidx_t] * npg))


def _sample_select_kernel(sp_ref, ss_ref, o_ref, s_ref, *, past, k, chunk):
    rows = sp_ref.shape[0]
    ncols = past + LANES
    s_ref[:, 0:past] = sp_ref[...]
    lane = lax.broadcasted_iota(I32, (rows, LANES), 1)
    s_ref[:, past:ncols] = jnp.where(lane == 0, ss_ref[...], -jnp.inf)
    n_adm = jnp.full((rows, 1), past + 1, I32)
    _select_to_bias(s_ref, o_ref, ncols // chunk, n_adm, key_axis=1, width=rows, chunk=chunk, k=k, nkeys=ncols)


def _sample_select(scores_past, score_self, *, k):
    nb, past = scores_past.shape
    ncols = past + LANES
    chunk = LANES * math.gcd(ncols // LANES, 5)
    return pl.pallas_call(
        functools.partial(_sample_select_kernel, past=past, k=k, chunk=chunk),
        out_shape=jax.ShapeDtypeStruct((nb, ncols), F32),
        scratch_shapes=[pltpu.VMEM((nb, ncols), F32)],
        compiler_params=pltpu.CompilerParams(vmem_limit_bytes=VMEM_LIMIT),
        name="index_select_sample",
    )(scores_past, score_self)


def _sample_attn_kernel(pt_ref, q_ref, bp_ref, bs_ref, ks_ref, vs_ref, *refs, npg):
    kpages, vpages = refs[:npg], refs[npg:2 * npg]
    o_ref, m_ref, l_ref, acc_ref = refs[2 * npg:]
    g = pl.program_id(1)

    @pl.when(g == 0)
    def _():
        m_ref[...] = jnp.full(m_ref.shape, NEG_BIAS, F32)
        l_ref[...] = jnp.zeros(l_ref.shape, F32)
        acc_ref[...] = jnp.zeros(acc_ref.shape, F32)

    head_of_lane = lax.broadcasted_iota(I32, (N_HEADS, D_ATTN), 1) // HEAD_DIM
    own = head_of_lane == lax.broadcasted_iota(I32, (N_HEADS, D_ATTN), 0)
    qrows = jnp.where(own, q_ref[0], 0.0)
    qb = qrows.astype(BF16)
    flat = lambda pg: pg[...].reshape(D_ATTN, PAGE_SIZE).astype(BF16)
    s = jnp.concatenate([jnp.dot(qb, flat(pg), preferred_element_type=F32) for pg in kpages], axis=1)
    s = s + bp_ref[0]
    m_prev = m_ref[...]
    m_new = jnp.maximum(m_prev, jnp.max(s, axis=1, keepdims=True))
    alpha = jnp.exp(m_prev - m_new)
    pm = jnp.exp(s - m_new)
    l_ref[...] = alpha * l_ref[...] + jnp.sum(pm, axis=1, keepdims=True)
    m_ref[...] = m_new
    pmb = pm.astype(BF16)
    nt = (((1,), (1,)), ((), ()))
    pv = None
    for i, pg in enumerate(vpages):
        part = lax.dot_general(pmb[:, i * PAGE_SIZE:(i + 1) * PAGE_SIZE], flat(pg), nt, preferred_element_type=F32)
        pv = part if pv is None else pv + part
    acc_ref[...] = alpha * acc_ref[...] + pv

    @pl.when(g == pl.num_programs(1) - 1)
    def _():
        rnd = lambda x: x.astype(BF16).astype(F32)
        s_self = jnp.sum(rnd(qrows) * rnd(ks_ref[0]), axis=1, keepdims=True) + bs_ref[0]
        m_prev = m_ref[...]
        m_new = jnp.maximum(m_prev, s_self)
        alpha = jnp.exp(m_prev - m_new)
        p_self = jnp.exp(s_self - m_new)
        l = alpha * l_ref[...] + p_self
        acc = alpha * acc_ref[...] + rnd(p_self) * rnd(vs_ref[0])
        o_ref[0] = jnp.sum(jnp.where(own, acc / l, 0.0), axis=0, keepdims=True).astype(o_ref.dtype)


def _sample_attention(page_table_flat, q_s, bias_past, bias_self, k_s, v_s, cache_k_t, cache_v_t, layer, *,
                      n_pages, npg):
    nb = q_s.shape[0]
    ngrp = n_pages // npg
    kv_specs = _page_specs((None, None, N_HEADS, HEAD_DIM, PAGE_SIZE), layer, n_pages, npg)
    per_seq = lambda n: pl.BlockSpec((1, 1, n), lambda b, g, pt: (b, 0, 0))
    return pl.pallas_call(
        functools.partial(_sample_attn_kernel, npg=npg),
        out_shape=jax.ShapeDtypeStruct((nb, 1, D_ATTN), BF16),
        grid_spec=pltpu.PrefetchScalarGridSpec(
            num_scalar_prefetch=1,
            grid=(nb, ngrp),
            in_specs=[per_seq(D_ATTN),
                      pl.BlockSpec((1, 1, npg * PAGE_SIZE), lambda b, g, pt: (b, 0, g)),
                      per_seq(1), per_seq(D_ATTN), per_seq(D_ATTN)] + kv_specs + kv_specs,
            out_specs=per_seq(D_ATTN),
            scratch_shapes=[pltpu.VMEM((N_HEADS, 1), F32), pltpu.VMEM((N_HEADS, 1), F32),
                            pltpu.VMEM((N_HEADS, D_ATTN), F32)],
        ),
        compiler_params=_cparams(("parallel", "arbitrary")),
        name="sparse_attention_sample",
    )(page_table_flat, q_s, bias_past, bias_self, k_s, v_s, *([cache_k_t] * npg), *([cache_v_t] * npg))


def _pack_layer(l, w_in, b_in, conv_w, conv_b, rg_wa, rg_ba, rg_wi, rg_bi, rg_lam, w_oa, w_ob, w_out,
                ln1_g, ln1_b, w_ff1, w_ff2, ln2_g, ln2_b, w_pe, w_pg, b_pg):
    d_rnn = conv_w.shape[-1]
    w, b = w_in[l], b_in[l]
    n_idx = 4 * D_ATTN
    ki0, wi0, r0 = n_idx, n_idx + IDX_DIM, n_idx + IDX_DIM + IDX_HEADS

    def pad_cols(a, n):
        return jnp.pad(a, ((0, 0), (0, n - a.shape[1])))

    b2 = b[None, :]
    w_head = jnp.concatenate([w[:, :n_idx], pad_cols(w[:, ki0:wi0], LANES), pad_cols(w[:, wi0:r0], LANES)], axis=1)
    b_head = jnp.concatenate([b2[:, :n_idx], pad_cols(b2[:, ki0:wi0], LANES), pad_cols(b2[:, wi0:r0], LANES)], axis=1)
    vec = lambda a: a[l][None, :]
    return dict(
        w_head=w_head.astype(BF16), b_head=b_head, w_rest=w[:, r0:].astype(BF16), b_rest=b2[:, r0:],
        conv_w=conv_w[l], conv_b=vec(conv_b), rg_wa=rg_wa[l].astype(BF16), rg_ba=vec(rg_ba),
        rg_wi=rg_wi[l].astype(BF16), rg_bi=vec(rg_bi), rg_lam=vec(rg_lam),
        w_oa=w_oa[l].astype(BF16), w_ob=w_ob[l].astype(BF16), w_out=w_out[l].astype(BF16),
        ln1_g=vec(ln1_g), ln1_b=vec(ln1_b), w_ff1=w_ff1[l].astype(BF16), w_ff2=w_ff2[l].astype(BF16),
        ln2_g=vec(ln2_g), ln2_b=vec(ln2_b), w_pe=w_pe[l].astype(BF16), w_pg=w_pg[l].astype(BF16), b_pg=vec(b_pg),
        d_rnn=d_rnn)


def kernel(x_prompt, x_sample, p_prompt, p_sample, cache_k, cache_v, cache_kidx, state_h, state_conv, page_table, w_in, b_in, conv_w, conv_b, rg_wa, rg_ba, rg_wi, rg_bi, rg_lam, w_oa, w_ob, w_out, ln1_g, ln1_b, w_ff1, w_ff2, ln2_g, ln2_b, w_pe, w_pg, b_pg):
    batch, seq, d_model = x_prompt.shape
    nb, dec_seq, _ = x_sample.shape
    assert dec_seq == 1
    depth = w_in.shape[0]
    d_rnn = conv_w.shape[-1]
    n_pages = page_table.shape[1]
    past = n_pages * PAGE_SIZE
    alpha = (2.0 * depth) ** 0.25
    k_prompt_sel = min(TOPK_MAX, seq // 4)
    k_sample_sel = min(TOPK_MAX, (past + dec_seq) // 4)
    m = batch * seq
    tm = min(256, m)
    npg = math.gcd(n_pages, 16)
    npg_idx = math.gcd(n_pages, 32)
    pt_flat = page_table.reshape(-1).astype(I32)
    cache_k_t = jnp.transpose(cache_k, (0, 1, 3, 4, 2))
    cache_v_t = jnp.transpose(cache_v, (0, 1, 3, 4, 2))
    cache_kidx_t = jnp.transpose(cache_kidx, (0, 1, 3, 2))

    xp = x_prompt.reshape(m, d_model)
    xs = x_sample.reshape(nb, d_model)
    outs = [[] for _ in range(10)]
    for l in range(depth):
        lw = _pack_layer(l, w_in, b_in, conv_w, conv_b, rg_wa, rg_ba, rg_wi, rg_bi, rg_lam, w_oa, w_ob, w_out,
                         ln1_g, ln1_b, w_ff1, w_ff2, ln2_g, ln2_b, w_pe, w_pg, b_pg)
        rg = (lw["conv_w"], lw["conv_b"], lw["rg_wa"], lw["rg_ba"], lw["rg_wi"], lw["rg_bi"], lw["rg_lam"])

        (qt4, kx, vt, k32, v32, qpt, kp, ki32, wit, xr, gr, ga, gb) = _inproj(
            xp, lw["w_head"], lw["b_head"], lw["w_rest"], lw["b_rest"], prompt=True, tm=tm)
        bias = _index_topk(qpt, wit, kp, batch=batch, seq=seq, k=k_prompt_sel)
        o_attn = _attention(qt4, kx, vt, bias, batch=batch, seq=seq)
        y_rnn, h_last = _rglru(xr, gr, *rg, batch=batch, seq=seq)
        xp = _post(xp, o_attn, y_rnn, ga, gb, p_prompt[l].reshape(m, -1), lw, alpha=alpha, tm=tm,
                   name="post_prompt")
        outs[0].append(k32.reshape(batch, seq, N_HEADS, HEAD_DIM))
        outs[1].append(v32.reshape(batch, seq, N_HEADS, HEAD_DIM))
        outs[2].append(ki32.reshape(batch, seq, IDX_DIM))
        outs[3].append(h_last.reshape(batch, d_rnn))
        outs[4].append(xr.reshape(batch, seq, d_rnn)[:, seq - (CONV_W - 1):])

        (q_s, k_s, v_s, qp_s, ki_s, wi_s, ss_s, xr_s, gr_s, ga_s, gb_s) = _inproj(
            xs, lw["w_head"], lw["b_head"], lw["w_rest"], lw["b_rest"], prompt=False, tm=nb)
        scores_past = _sample_scores(pt_flat, jnp.swapaxes(qp_s, 0, 1), wi_s.reshape(nb, IDX_HEADS, 1),
                                     cache_kidx_t, l, n_pages=n_pages, npg=npg_idx)
        bias_s = _sample_select(scores_past.reshape(nb, past), ss_s, k=k_sample_sel)
        o_s = _sample_attention(
            pt_flat, q_s.reshape(nb, 1, D_ATTN), bias_s[:, :past].reshape(nb, 1, past),
            bias_s[:, past:past + 1].reshape(nb, 1, 1), k_s.reshape(nb, 1, D_ATTN), v_s.reshape(nb, 1, D_ATTN),
            cache_k_t, cache_v_t, l, n_pages=n_pages, npg=npg)
        y_s, h_s = _rglru_step(xr_s, gr_s, jnp.swapaxes(state_conv[l], 0, 1), state_h[l], *rg)
        xs = _post(xs, o_s.reshape(nb, D_ATTN), y_s, ga_s, gb_s, p_sample[l].reshape(nb, -1), lw, alpha=alpha,
                   tm=nb, name="post_sample")
        outs[5].append(k_s.reshape(nb, 1, N_HEADS, HEAD_DIM))
        outs[6].append(v_s.reshape(nb, 1, N_HEADS, HEAD_DIM))
        outs[7].append(ki_s.reshape(nb, 1, IDX_DIM))
        outs[8].append(h_s)
        outs[9].append(jnp.concatenate([state_conv[l][:, 1:], xr_s[:, None, :]], axis=1))

    return (xp.reshape(batch, seq, d_model), xs.reshape(nb, 1, d_model)) + tuple(jnp.stack(o) for o in outs)
```

```python
import functools
import math

import numpy as np
import jax
import jax.numpy as jnp
from jax import lax
from jax.experimental import pallas as pl
from jax.experimental.pallas import tpu as pltpu

F32 = jnp.float32
BF16 = jnp.bfloat16
I32 = jnp.int32

N_HEADS = 8
HEAD_DIM = 64
D_ATTN = N_HEADS * HEAD_DIM
IDX_HEADS = 8
IDX_DIM = 64
TOPK_MAX = 256
PAGE_SIZE = 128
RNN_BLOCKS = 8
CONV_W = 4
RG_C = 8.0
LN_EPS = 1e-5

LANES = 128
VMEM_LIMIT = 56 * 1024 * 1024
NEG_BIAS = -1e30
QUERY_BLOCK = 256
LOG2E = math.log2(math.e)
F32_LOWEST = float(np.finfo(np.float32).min)

_SEG = {}
_off = 0
for _name, _n in (("q", D_ATTN), ("k", D_ATTN), ("v", D_ATTN), ("qi", IDX_HEADS * IDX_DIM),
                  ("ki", LANES), ("wi", LANES)):
    _SEG[_name] = (_off, _off + _n)
    _off += _n


def _cparams(sem):
    return pltpu.CompilerParams(dimension_semantics=sem, vmem_limit_bytes=VMEM_LIMIT)


def _resident(shape):
    nd = len(shape)
    return pl.BlockSpec(shape, lambda *_: (0,) * nd, pipeline_mode=pl.Buffered(1))


def _sigmoid(x):
    return 1.0 / (1.0 + jnp.exp(-x))


def _gelu_tanh(x):
    c = math.sqrt(2.0 / math.pi)
    return 0.5 * x * (1.0 + jnp.tanh(c * (x + 0.044715 * (x * x * x))))


def _expm1(x):
    u = jnp.exp(x)
    um1 = u - 1.0
    return jnp.where(u == 1.0, x, jnp.where(x < -0.5, um1, um1 * x / jnp.log(u)))


def _layer_norm(x, g, b):
    mu = jnp.mean(x, axis=-1, keepdims=True)
    xc = x - mu
    var = jnp.mean(xc * xc, axis=-1, keepdims=True)
    return xc * lax.rsqrt(var + LN_EPS) * g + b


def _split_hi_lo(x):
    hi = x.astype(BF16)
    lo = (x - hi.astype(F32)).astype(BF16)
    return hi, lo


def _swap_halves(t):
    return jnp.concatenate([t[:, HEAD_DIM:], t[:, :HEAD_DIM]], axis=1)


def _inproj_kernel(x_ref, w_ref, b_ref, wr_ref, br_ref, *out_refs, prompt, d_rnn):
    if prompt:
        (qt_ref, kx_ref, vt_ref, k_ref, v_ref, qpt_ref, kp_ref, ki_ref, wit_ref,
         xr_ref, gr_ref, ga_ref, gb_ref) = out_refs
    else:
        (q_ref, k_ref, v_ref, qp_ref, ki_ref, wi_ref, ss_ref,
         xr_ref, gr_ref, ga_ref, gb_ref) = out_refs
    xb = x_ref[...].astype(BF16)
    tm = xb.shape[0]

    def seg(name):
        a, b = _SEG[name]
        return jnp.dot(xb, w_ref[:, a:b], preferred_element_type=F32) + b_ref[:, a:b]

    lane = lax.broadcasted_iota(I32, (tm, LANES), 1)
    low = lane < HEAD_DIM

    zq = seg("q") * (HEAD_DIM ** -0.5 * (LOG2E if prompt else 1.0))
    if prompt:
        for p in range(N_HEADS // 2):
            qt_ref[p] = zq[:, p * LANES:(p + 1) * LANES].T.astype(BF16)
    else:
        q_ref[...] = zq

    zk = seg("k")
    k_ref[...] = zk
    zv = seg("v")
    v_ref[...] = zv
    if prompt:
        zkb = zk.astype(BF16)
        zero_b = jnp.zeros((tm, LANES), BF16)
        for p in range(N_HEADS // 2):
            kt = zkb[:, p * LANES:(p + 1) * LANES]
            kx_ref[2 * p] = jnp.where(low, kt, zero_b)
            kx_ref[2 * p + 1] = jnp.where(low, zero_b, kt)
            vt_ref[p * LANES:(p + 1) * LANES, :] = zv[:, p * LANES:(p + 1) * LANES].T.astype(BF16)

    zqi = seg("qi") * (IDX_DIM ** -0.5)
    zki = seg("ki")
    ki_ref[...] = zki[:, :IDX_DIM]
    zwi = seg("wi") * (IDX_HEADS ** -0.5)
    kdup = jnp.where(low, zki, _swap_halves(zki))
    if prompt:
        wit_ref[...] = zwi.T[:IDX_HEADS]
        khi, klo = _split_hi_lo(kdup)
        khl = jnp.where(low, khi, klo)
        kp_ref[:, 0:LANES] = khl
        kp_ref[:, LANES:2 * LANES] = khl
    else:
        wi_ref[...] = zwi[:, :IDX_HEADS]
    qblk = min(QUERY_BLOCK, tm)
    n_qblk = tm // qblk if prompt else 0
    for p in range(IDX_HEADS // 2):
        t = zqi[:, p * LANES:(p + 1) * LANES]
        sw = _swap_halves(t)
        for e in range(2):
            h = 2 * p + e
            dup = jnp.where(low, t, sw) if e == 0 else jnp.where(low, sw, t)
            if prompt:
                hi, lo = _split_hi_lo(dup.T)
                for r in range(n_qblk):
                    c0 = (r * IDX_HEADS + h) * qblk
                    qpt_ref[0:LANES, c0:c0 + qblk] = hi[:, r * qblk:(r + 1) * qblk]
                    qpt_ref[LANES:2 * LANES, c0:c0 + qblk] = lo[:, r * qblk:(r + 1) * qblk]
            else:
                hi, lo = _split_hi_lo(dup)
                qp_ref[h, :, 0:LANES] = hi
                qp_ref[h, :, LANES:2 * LANES] = lo
    if not prompt:
        acc = jnp.zeros((tm, 1), F32)
        for p in range(IDX_HEADS // 2):
            prod = zqi[:, p * LANES:(p + 1) * LANES] * kdup
            for e in range(2):
                h = 2 * p + e
                keep = low if e == 0 else jnp.logical_not(low)
                s_h = jnp.sum(jnp.where(keep, prod, 0.0), axis=1, keepdims=True)
                acc = acc + zwi[:, h:h + 1] * jnp.maximum(s_h, 0.0)
        ss_ref[...] = acc

    for i, ref in enumerate((xr_ref, gr_ref, ga_ref, gb_ref)):
        a, b = i * d_rnn, (i + 1) * d_rnn
        ref[...] = jnp.dot(xb, wr_ref[:, a:b], preferred_element_type=F32) + br_ref[:, a:b]


def _inproj(x2d, w_head, b_head, w_rest, b_rest, *, prompt, tm):
    m, d = x2d.shape
    d_rnn = w_rest.shape[1] // 4
    sds = jax.ShapeDtypeStruct
    row = lambda n: pl.BlockSpec((tm, n), lambda i: (i, 0))
    col = lambda n: pl.BlockSpec((n, tm), lambda i: (0, i))
    wide = [sds((m, d_rnn), F32)] * 4
    wide_specs = [row(d_rnn)] * 4
    if prompt:
        qpt_cols = IDX_HEADS * tm
        out_shape = [sds((N_HEADS // 2, LANES, m), BF16), sds((N_HEADS, m, LANES), BF16),
                     sds((D_ATTN, m), BF16), sds((m, D_ATTN), F32), sds((m, D_ATTN), F32),
                     sds((2 * LANES, IDX_HEADS * m), BF16), sds((m, 2 * LANES), BF16),
                     sds((m, IDX_DIM), F32), sds((IDX_HEADS, m), F32)] + wide
        out_specs = [pl.BlockSpec((N_HEADS // 2, LANES, tm), lambda i: (0, 0, i)),
                     pl.BlockSpec((N_HEADS, tm, LANES), lambda i: (0, i, 0)),
                     col(D_ATTN), row(D_ATTN), row(D_ATTN),
                     pl.BlockSpec((2 * LANES, qpt_cols), lambda i: (0, i)), row(2 * LANES),
                     row(IDX_DIM), col(IDX_HEADS)] + wide_specs
    else:
        out_shape = [sds((m, D_ATTN), F32), sds((m, D_ATTN), F32), sds((m, D_ATTN), F32),
                     sds((IDX_HEADS, m, 2 * LANES), BF16), sds((m, IDX_DIM), F32), sds((m, IDX_HEADS), F32),
                     sds((m, 1), F32)] + wide
        out_specs = [row(D_ATTN), row(D_ATTN), row(D_ATTN),
                     pl.BlockSpec((IDX_HEADS, tm, 2 * LANES), lambda i: (0, i, 0)),
                     row(IDX_DIM), row(IDX_HEADS), row(1)] + wide_specs
    return pl.pallas_call(
        functools.partial(_inproj_kernel, prompt=prompt, d_rnn=d_rnn),
        out_shape=out_shape,
        grid=(m // tm,),
        in_specs=[row(d), _resident(w_head.shape), _resident(b_head.shape),
                  _resident(w_rest.shape), _resident(b_rest.shape)],
        out_specs=out_specs,
        compiler_params=_cparams(("parallel",)),
        name="inproj_prompt" if prompt else "inproj_sample",
    )(x2d, w_head, b_head, w_rest, b_rest)


def _ukey_to_f32(u):
    sk = u ^ jnp.int32(-2 ** 31)
    bits = jnp.where(sk >= 0, sk, sk ^ jnp.int32(2 ** 31 - 1))
    return lax.bitcast_convert_type(bits, F32)


def _select_to_bias(s_ref, o_ref, nch, n_adm, *, key_axis, width, chunk, k, nkeys, bounds=None):
    kf = float(k)
    sub = 64 if key_axis == 0 else LANES
    tiles = chunk // sub
    tshape = (sub, width) if key_axis == 0 else (width, sub)
    stat_shape = (1, width) if key_axis == 0 else (width, 1)

    def tile(ref, o):
        return ref.at[pl.ds(o, sub), :] if key_axis == 0 else ref.at[:, pl.ds(o, sub)]

    def count(pred):
        def body(c, acc):
            off = pl.multiple_of(c * chunk, chunk)
            for u in range(tiles):
                o = off + u * sub
                acc = acc + pred(tile(s_ref, o)[...], o)
            return acc
        acc = lax.fori_loop(0, nch, body, jnp.zeros(tshape, F32))
        return jnp.sum(acc, axis=key_axis, keepdims=True)

    def wide(v):
        return jnp.broadcast_to(v, tshape)

    search = n_adm > k
    done0 = jnp.where(search, 0.0, 1.0).astype(F32)

    def count_ge(v):
        v_w = wide(v)
        return count(lambda t, o: jnp.where(t >= v_w, 1.0, 0.0))

    def all_done(done):
        return jnp.sum(done) >= float(width)

    if bounds is None:
        def bit_body(st):
            i, cur, done = st
            trial = cur | lax.shift_left(jnp.int32(1), 31 - i)
            cnt = count_ge(_ukey_to_f32(trial))
            cur = jnp.where(jnp.logical_and(done == 0.0, cnt >= kf), trial, cur)
            return i + 1, cur, jnp.where(cnt == kf, 1.0, done)

        _, cur, _ = lax.while_loop(lambda st: jnp.logical_and(st[0] < 32, jnp.logical_not(all_done(st[2]))),
                                   bit_body, (jnp.int32(0), jnp.zeros(stat_shape, I32), done0))
        kth = _ukey_to_f32(cur)
    else:
        lo, hi = bounds
        hi = hi + 0.0
        hb = lax.bitcast_convert_type(hi, I32)
        hi = lax.bitcast_convert_type(jnp.where(hi >= 0.0, hb + 1, hb - 1), F32)
        zero = jnp.zeros(stat_shape, F32)
        n_ge0 = count_ge(zero)
        n_gt0 = count(lambda t, o: jnp.where(t > 0.0, 1.0, 0.0))
        at_zero = jnp.logical_and(n_ge0 >= kf, n_gt0 < kf)
        lo = jnp.where(n_ge0 >= kf, jnp.maximum(lo, 0.0), lo)
        hi = jnp.where(n_ge0 >= kf, hi, jnp.minimum(hi, 0.0))
        done = jnp.where(at_zero, 1.0, done0)
        kth0 = jnp.where(at_zero, 0.0, lo)
        ckth0 = jnp.where(at_zero, n_ge0, jnp.inf)

        def bis_body(st):
            i, lo, hi, kth, ckth, done = st
            mid = 0.5 * lo + 0.5 * hi
            stuck = jnp.logical_or(mid <= lo, mid >= hi)
            cnt = count_ge(mid)
            open_ = jnp.logical_and(done == 0.0, jnp.logical_not(stuck))
            up = jnp.logical_and(open_, cnt >= kf)
            down = jnp.logical_and(open_, cnt < kf)
            done = jnp.where(jnp.logical_or(stuck, cnt == kf), 1.0, done)
            return (i + 1, jnp.where(up, mid, lo), jnp.where(down, mid, hi), jnp.where(up, mid, kth),
                    jnp.where(up, cnt, ckth), done)

        st = lax.fori_loop(0, 16, lambda _, st: bis_body(st), (jnp.int32(0), lo, hi, kth0, ckth0, done))
        st = lax.while_loop(lambda st: jnp.logical_and(st[0] < 320, jnp.logical_not(all_done(st[5]))), bis_body, st)
        kth, cnt_ge = st[3], st[4]
    thr = jnp.where(search, kth, F32_LOWEST)
    thr_w = wide(thr)
    if bounds is None:
        cnt_ge = count_ge(thr)
    tie = jnp.logical_and(search, cnt_ge > kf)
    n_tie = jnp.sum(jnp.where(tie, 1.0, 0.0))

    def write(bias_fn):
        def body(c, carry):
            off = pl.multiple_of(c * chunk, chunk)
            for u in range(tiles):
                o = off + u * sub
                tile(o_ref, o)[...] = bias_fn(tile(s_ref, o)[...], o).astype(o_ref.dtype)
            return carry
        lax.fori_loop(0, nch, body, 0)

    @pl.when(n_tie == 0.0)
    def _():
        write(lambda t, o: jnp.where(t >= thr_w, 0.0, NEG_BIAS))

    @pl.when(n_tie > 0.0)
    def _():
        cnt_gt = count(lambda t, o: jnp.where(t > thr_w, 1.0, 0.0))
        need = jnp.where(search, kf - cnt_gt, float(2 * nkeys))
        tc = chunk if key_axis == 0 else LANES
        ri = lax.broadcasted_iota(I32, (tc, tc), 0)
        ci = lax.broadcasted_iota(I32, (tc, tc), 1)
        tri = jnp.where((ri >= ci) if key_axis == 0 else (ri <= ci), 1.0, 0.0).astype(BF16)

        def body(c, seen):
            off = pl.multiple_of(c * tc, tc)
            src = s_ref.at[pl.ds(off, tc), :] if key_axis == 0 else s_ref.at[:, pl.ds(off, tc)]
            dst = o_ref.at[pl.ds(off, tc), :] if key_axis == 0 else o_ref.at[:, pl.ds(off, tc)]
            t = src[...]
            eq = t == thr
            eqb = jnp.where(eq, 1.0, 0.0).astype(BF16)
            if key_axis == 0:
                rank = jnp.dot(tri, eqb, preferred_element_type=F32) + seen
                last = rank[tc - 1:tc, :]
            else:
                rank = jnp.dot(eqb, tri, preferred_element_type=F32) + seen
                last = seen + jnp.sum(jnp.where(eq, 1.0, 0.0), axis=1, keepdims=True)
            dst[...] = jnp.where(eq, jnp.where(rank <= need, 0.0, NEG_BIAS),
                                 jnp.where(t > thr, 0.0, NEG_BIAS)).astype(o_ref.dtype)
            return last

        lax.fori_loop(0, nch * (chunk // tc), body, jnp.zeros(stat_shape, F32))


def _index_topk_kernel(qpt_ref, wit_ref, kp_ref, o_ref, s_ref, *, seq, cols, schunk, chunk, k):
    j = pl.program_id(1)
    nkeys = j * cols + cols
    nch = (nkeys + chunk - 1) // chunk
    nsc = (nkeys + schunk - 1) // schunk
    w = wit_ref[...]
    qpos = j * cols + lax.broadcasted_iota(I32, (schunk, cols), 1)
    krow = lax.broadcasted_iota(I32, (schunk, cols), 0)

    def score_body(c, carry):
        off = pl.multiple_of(c * schunk, schunk)
        kc = kp_ref[pl.ds(off, schunk), :]
        acc = None
        for pr in range(IDX_HEADS // 2):
            st = jnp.dot(kc, qpt_ref[:, pr * 2 * cols:(pr + 1) * 2 * cols], preferred_element_type=F32)
            for e in range(2):
                h = 2 * pr + e
                term = w[h:h + 1, :] * jnp.maximum(st[:, e * cols:(e + 1) * cols], 0.0)
                acc = term if acc is None else acc + term
        s_ref[pl.ds(off, schunk), :] = jnp.where(krow + off <= qpos, acc, -jnp.inf)
        return carry

    lax.fori_loop(0, nsc, score_body, 0)
    n_adm = j * cols + lax.broadcasted_iota(I32, (1, cols), 1) + 1
    bounds = None
    if cols >= k:
        def fold(c, mx):
            return jnp.maximum(mx, s_ref[pl.ds(pl.multiple_of(c * cols, cols), cols), :])
        class_max = lax.fori_loop(0, nkeys // cols, fold, jnp.full((cols, cols), -jnp.inf, F32))
        bounds = (jnp.min(class_max, axis=0, keepdims=True), jnp.max(class_max, axis=0, keepdims=True))
    _select_to_bias(s_ref, o_ref, nch, n_adm, key_axis=0, width=cols, chunk=chunk, k=k, nkeys=seq, bounds=bounds)

    def fill(c, carry):
        off = pl.multiple_of(c * chunk, chunk)
        o_ref[pl.ds(off, chunk), :] = jnp.full((chunk, cols), NEG_BIAS, o_ref.dtype)
        return carry

    lax.fori_loop(nch, seq // chunk, fill, 0)


def _index_topk(qpt, wit, kp, *, batch, seq, k):
    cols = min(QUERY_BLOCK, seq)
    chunk = min(512, seq)
    schunk = min(1024, seq)
    nqb = seq // cols
    return pl.pallas_call(
        functools.partial(_index_topk_kernel, seq=seq, cols=cols, schunk=schunk, chunk=chunk, k=k),
        out_shape=jax.ShapeDtypeStruct((batch, seq, seq), BF16),
        grid=(batch, nqb),
        in_specs=[pl.BlockSpec((2 * LANES, IDX_HEADS * cols), lambda b, j: (0, b * nqb + j)),
                  pl.BlockSpec((IDX_HEADS, cols), lambda b, j: (0, b * nqb + j)),
                  pl.BlockSpec((seq, 2 * LANES), lambda b, j: (b, 0))],
        out_specs=pl.BlockSpec((None, seq, cols), lambda b, j: (b, 0, j)),
        scratch_shapes=[pltpu.VMEM((seq, cols), F32)],
        compiler_params=_cparams(("parallel", "arbitrary")),
        name="index_topk_prompt",
    )(qpt, wit, kp)


def _attn_kernel(qt_tab, kt_tab, qt_ref, kx_ref, vt_ref, b_ref, o_ref, m_ref, l_ref, acc_ref, *, tq, tk):
    p = pl.program_id(1)
    kt = kt_tab[p]
    last = (qt_tab[p] * tq + tq - 1) // tk

    @pl.when(kt == 0)
    def _():
        m_ref[...] = jnp.full(m_ref.shape, NEG_BIAS, F32)
        l_ref[...] = jnp.zeros(l_ref.shape, F32)
        acc_ref[...] = jnp.zeros(acc_ref.shape, F32)

    def over_keys(op, x):
        slab = max(x.shape[0] // 2, 8)
        parts = [x[i:i + slab] for i in range(0, x.shape[0], slab)]
        while len(parts) > 1:
            parts = [op(parts[i], parts[i + 1]) for i in range(0, len(parts), 2)]
        return (jnp.max if op is jnp.maximum else jnp.sum)(parts[0], axis=0, keepdims=True)

    bias = b_ref[...].astype(F32)
    scores = [jnp.dot(kx_ref[h], qt_ref[h // 2], preferred_element_type=F32) + bias for h in range(N_HEADS)]
    for h, s in enumerate(scores):
        rows = slice(h * HEAD_DIM, (h + 1) * HEAD_DIM)
        m_prev = m_ref[h:h + 1, :]
        m_new = jnp.maximum(m_prev, over_keys(jnp.maximum, s))
        alpha = jnp.exp2(m_prev - m_new)
        pm = jnp.exp2(s - m_new)
        l_ref[h:h + 1, :] = alpha * l_ref[h:h + 1, :] + over_keys(jnp.add, pm)
        m_ref[h:h + 1, :] = m_new
        acc_ref[rows, :] = alpha * acc_ref[rows, :] + jnp.dot(vt_ref[rows, :], pm.astype(BF16),
                                                              preferred_element_type=F32)

    @pl.when(kt == last)
    def _():
        for h in range(N_HEADS):
            rows = slice(h * HEAD_DIM, (h + 1) * HEAD_DIM)
            acc_ref[rows, :] = acc_ref[rows, :] / l_ref[h:h + 1, :]
        o_ref[...] = acc_ref[...].T.astype(o_ref.dtype)


def _attention(qt4, kx, vt, bias, *, batch, seq):
    tq = min(512, seq)
    tk = min(512, seq)
    nq, nk = seq // tq, seq // tk
    qt, kt = [], []
    for i in range(nq):
        for c in range((i * tq + tq - 1) // tk + 1):
            qt.append(i)
            kt.append(c)
    npairs = len(qt)
    m = batch * seq
    return pl.pallas_call(
        functools.partial(_attn_kernel, tq=tq, tk=tk),
        out_shape=jax.ShapeDtypeStruct((m, D_ATTN), BF16),
        grid_spec=pltpu.PrefetchScalarGridSpec(
            num_scalar_prefetch=2,
            grid=(batch, npairs),
            in_specs=[
                pl.BlockSpec((N_HEADS // 2, LANES, tq), lambda b, p, qt, kt: (0, 0, b * nq + qt[p])),
                pl.BlockSpec((N_HEADS, tk, LANES), lambda b, p, qt, kt: (0, b * nk + kt[p], 0)),
                pl.BlockSpec((D_ATTN, tk), lambda b, p, qt, kt: (0, b * nk + kt[p])),
                pl.BlockSpec((None, tk, tq), lambda b, p, qt, kt: (b, kt[p], qt[p])),
            ],
            out_specs=pl.BlockSpec((tq, D_ATTN), lambda b, p, qt, kt: (b * nq + qt[p], 0)),
            scratch_shapes=[pltpu.VMEM((N_HEADS, tq), F32), pltpu.VMEM((N_HEADS, tq), F32),
                            pltpu.VMEM((D_ATTN, tq), F32)],
        ),
        compiler_params=_cparams(("parallel", "arbitrary")),
        name="sparse_attention_prompt",
    )(jnp.asarray(qt, I32), jnp.asarray(kt, I32), qt4, kx, vt, bias)


def _rglru_gates(xc, wa_ref, ba_ref, wg_ref, bg_ref, lam_ref, store):
    xcb = xc.astype(BF16)
    lam = lam_ref[...]
    nlam = -lam
    softplus = jnp.maximum(nlam, 0.0) + jnp.log1p(jnp.exp(-jnp.abs(nlam)))
    bw = xc.shape[1] // RNN_BLOCKS
    for n in range(RNN_BLOCKS):
        sl = slice(n * bw, (n + 1) * bw)
        r = _sigmoid(jnp.dot(xcb[:, sl], wa_ref[n], preferred_element_type=F32) + ba_ref[:, sl])
        ig = _sigmoid(jnp.dot(xcb[:, sl], wg_ref[n], preferred_element_type=F32) + bg_ref[:, sl])
        log_a = (-RG_C) * r * softplus[:, sl]
        a = jnp.exp(log_a)
        bx = jnp.sqrt(-_expm1(2.0 * log_a)) * (ig * xc[:, sl])
        store(sl, a, bx)


def _rglru_kernel(xr_ref, gr_ref, cw_ref, cb_ref, wa_ref, ba_ref, wg_ref, bg_ref, lam_ref,
                  y_ref, hl_ref, xbuf, a_s, b_s, h_s, *, tc):
    c = pl.program_id(1)
    pad = 8

    @pl.when(c == 0)
    def _():
        xbuf[0:pad] = jnp.zeros((pad, xbuf.shape[1]), F32)
        h_s[...] = jnp.zeros(h_s.shape, F32)

    x = xr_ref[...]
    xbuf[pad:pad + tc] = x
    cw = cw_ref[...]
    xc = cb_ref[...] + cw[CONV_W - 1:CONV_W] * x
    for jj in range(1, CONV_W):
        xc = xc + cw[CONV_W - 1 - jj:CONV_W - jj] * xbuf[pad - jj:pad - jj + tc]
    xbuf[0:pad] = x[tc - pad:tc]

    def store(sl, a, bx):
        a_s[:, sl] = a
        b_s[:, sl] = bx

    _rglru_gates(xc, wa_ref, ba_ref, wg_ref, bg_ref, lam_ref, store)

    def step(t, h):
        h = a_s[pl.ds(t, 1), :] * h + b_s[pl.ds(t, 1), :]
        b_s[pl.ds(t, 1), :] = h
        return h

    h = lax.fori_loop(0, tc, step, h_s[...], unroll=8)
    h_s[...] = h
    hl_ref[0] = h
    y_ref[...] = (b_s[...] * _gelu_tanh(gr_ref[...])).astype(y_ref.dtype)


def _rglru(xr, gr, cw, cb, wa, ba, wg, bg, lam, *, batch, seq):
    tc = min(512, seq)
    nc = seq // tc
    m, c = xr.shape
    row = pl.BlockSpec((tc, c), lambda b, i: (b * nc + i, 0))
    vec = pl.BlockSpec((1, c), lambda b, i: (0, 0))
    blk = pl.BlockSpec(wa.shape, lambda b, i: (0, 0, 0))
    return pl.pallas_call(
        functools.partial(_rglru_kernel, tc=tc),
        out_shape=[jax.ShapeDtypeStruct((m, c), BF16), jax.ShapeDtypeStruct((batch, 1, c), F32)],
        grid=(batch, nc),
        in_specs=[row, row, pl.BlockSpec((CONV_W, c), lambda b, i: (0, 0)), vec, blk, vec, blk, vec, vec],
        out_specs=[row, pl.BlockSpec((1, 1, c), lambda b, i: (b, 0, 0))],
        scratch_shapes=[pltpu.VMEM((tc + 8, c), F32), pltpu.VMEM((tc, c), F32), pltpu.VMEM((tc, c), F32),
                        pltpu.VMEM((1, c), F32)],
        compiler_params=_cparams(("parallel", "arbitrary")),
        name="rglru_prompt",
    )(xr, gr, cw, cb, wa, ba, wg, bg, lam)


def _rglru_step_kernel(xr_ref, gr_ref, sc_ref, h0_ref, cw_ref, cb_ref, wa_ref, ba_ref, wg_ref, bg_ref, lam_ref,
                       y_ref, h_ref):
    x = xr_ref[...]
    cw = cw_ref[...]
    xc = cb_ref[...] + cw[CONV_W - 1:CONV_W] * x
    for jj in range(CONV_W - 1):
        xc = xc + cw[jj:jj + 1] * sc_ref[jj]
    h0 = h0_ref[...]

    def store(sl, a, bx):
        h_ref[:, sl] = a * h0[:, sl] + bx

    _rglru_gates(xc, wa_ref, ba_ref, wg_ref, bg_ref, lam_ref, store)
    y_ref[...] = (h_ref[...] * _gelu_tanh(gr_ref[...])).astype(y_ref.dtype)


def _rglru_step(xr, gr, sc, h0, cw, cb, wa, ba, wg, bg, lam):
    m, c = xr.shape
    return pl.pallas_call(
        _rglru_step_kernel,
        out_shape=[jax.ShapeDtypeStruct((m, c), BF16), jax.ShapeDtypeStruct((m, c), F32)],
        compiler_params=pltpu.CompilerParams(vmem_limit_bytes=VMEM_LIMIT),
        name="rglru_sample",
    )(xr, gr, sc, h0, cw, cb, wa, ba, wg, bg, lam)


def _post_kernel(x_ref, o_ref, y_ref, ga_ref, gb_ref, pe_ref,
                 woa_ref, wob_ref, wout_ref, g1_ref, b1_ref, wf1_ref, wf2_ref, g2_ref, b2_ref,
                 wpe_ref, wpg_ref, bpg_ref, out_ref, *, alpha, ff_chunk):
    dot = functools.partial(jnp.dot, preferred_element_type=F32)
    merged = (_sigmoid(ga_ref[...]) * dot(o_ref[...], woa_ref[...])
              + _sigmoid(gb_ref[...]) * dot(y_ref[...], wob_ref[...]))
    x1 = _layer_norm(alpha * x_ref[...] + dot(merged.astype(BF16), wout_ref[...]), g1_ref[...], b1_ref[...])
    x1b = x1.astype(BF16)
    d_ff = wf1_ref.shape[1]
    ff = None
    for c0 in range(0, d_ff, ff_chunk):
        hcol = jnp.maximum(dot(x1b, wf1_ref[:, c0:c0 + ff_chunk]), 0.0)
        part = dot((hcol * hcol).astype(BF16), wf2_ref[c0:c0 + ff_chunk, :])
        ff = part if ff is None else ff + part
    x2 = _layer_norm(alpha * x1 + ff, g2_ref[...], b2_ref[...])
    gate = _sigmoid(dot(x2.astype(BF16), wpg_ref[...]) + bpg_ref[...])
    out_ref[...] = x2 + gate * dot(pe_ref[...].astype(BF16), wpe_ref[...])


def _post(x, o, y, ga, gb, pe, lw, *, alpha, tm, name):
    m, d = x.shape
    row = lambda n: pl.BlockSpec((tm, n), lambda i: (i, 0))
    weights = (lw["w_oa"], lw["w_ob"], lw["w_out"], lw["ln1_g"], lw["ln1_b"], lw["w_ff1"], lw["w_ff2"],
               lw["ln2_g"], lw["ln2_b"], lw["w_pe"], lw["w_pg"], lw["b_pg"])
    return pl.pallas_call(
        functools.partial(_post_kernel, alpha=alpha, ff_chunk=min(1024, lw["w_ff1"].shape[1])),
        out_shape=jax.ShapeDtypeStruct((m, d), F32),
        grid=(m // tm,),
        in_specs=[row(d), row(o.shape[1]), row(y.shape[1]), row(d), row(d), row(pe.shape[1])]
        + [_resident(w.shape) for w in weights],
        out_specs=row(d),
        compiler_params=_cparams(("parallel",)),
        name=name,
    )(x, o, y, ga, gb, pe, *weights)


def _page_specs(block, layer, n_pages, npg):
    nd = len(block)

    def index_map(b, g, pt, i):
        return (layer, pt[b * n_pages + g * npg + i]) + (0,) * (nd - 2)

    return [pl.BlockSpec(block, functools.partial(index_map, i=i)) for i in range(npg)]


def _sample_scores_kernel(pt_ref, qp_ref, w_ref, *refs, npg):
    pages, o_ref = refs[:npg], refs[npg]
    qp = qp_ref[0]
    q2 = jnp.concatenate([qp[:, 0:IDX_DIM], qp[:, 2 * IDX_DIM:3 * IDX_DIM]], axis=0)
    w = w_ref[0]
    for i, pg in enumerate(pages):
        kh, kl = _split_hi_lo(pg[...])
        ab = jnp.dot(q2, kh, preferred_element_type=F32) + jnp.dot(q2, kl, preferred_element_type=F32)
        s = ab[0:IDX_HEADS] + ab[IDX_HEADS:]
        o_ref[0, :, i * PAGE_SIZE:(i + 1) * PAGE_SIZE] = jnp.sum(w * jnp.maximum(s, 0.0), axis=0, keepdims=True)


def _sample_scores(page_table_flat, qp_s, wi_s, cache_kidx_t, layer, *, n_pages, npg):
    nb = qp_s.shape[0]
    ngrp = n_pages // npg
    return pl.pallas_call(
        functools.partial(_sample_scores_kernel, npg=npg),
        out_shape=jax.ShapeDtypeStruct((nb, 1, n_pages * PAGE_SIZE), F32),
        grid_spec=pltpu.PrefetchScalarGridSpec(
            num_scalar_prefetch=1,
            grid=(nb, ngrp),
            in_specs=[pl.BlockSpec((1, IDX_HEADS, 2 * LANES), lambda b, g, pt: (b, 0, 0)),
                      pl.BlockSpec((1, IDX_HEADS, 1), lambda b, g, pt: (b, 0, 0))]
            + _page_specs((None, None, IDX_DIM, PAGE_SIZE), layer, n_pages, npg),
            out_specs=pl.BlockSpec((1, 1, npg * PAGE_SIZE), lambda b, g, pt: (b, 0, g)),
        ),
        compiler_params=_cparams(("parallel", "arbitrary")),
        name="index_scores_sample",
    )(page_table_flat, qp_s, wi_s, *([cache_kidx_t] * npg))


def _sample_select_kernel(sp_ref, ss_ref, o_ref, s_ref, *, past, k, chunk):
    rows = sp_ref.shape[0]
    ncols = past + LANES
    s_ref[:, 0:past] = sp_ref[...]
    lane = lax.broadcasted_iota(I32, (rows, LANES), 1)
    s_ref[:, past:ncols] = jnp.where(lane == 0, ss_ref[...], -jnp.inf)
    n_adm = jnp.full((rows, 1), past + 1, I32)
    _select_to_bias(s_ref, o_ref, ncols // chunk, n_adm, key_axis=1, width=rows, chunk=chunk, k=k, nkeys=ncols)


def _sample_select(scores_past, score_self, *, k):
    nb, past = scores_past.shape
    ncols = past + LANES
    chunk = LANES * math.gcd(ncols // LANES, 5)
    return pl.pallas_call(
        functools.partial(_sample_select_kernel, past=past, k=k, chunk=chunk),
        out_shape=jax.ShapeDtypeStruct((nb, ncols), F32),
        scratch_shapes=[pltpu.VMEM((nb, ncols), F32)],
        compiler_params=pltpu.CompilerParams(vmem_limit_bytes=VMEM_LIMIT),
        name="index_select_sample",
    )(scores_past, score_self)


def _sample_attn_kernel(pt_ref, q_ref, bp_ref, bs_ref, ks_ref, vs_ref, *refs, npg):
    kpages, vpages = refs[:npg], refs[npg:2 * npg]
    o_ref, m_ref, l_ref, acc_ref = refs[2 * npg:]
    g = pl.program_id(1)

    @pl.when(g == 0)
    def _():
        m_ref[...] = jnp.full(m_ref.shape, NEG_BIAS, F32)
        l_ref[...] = jnp.zeros(l_ref.shape, F32)
        acc_ref[...] = jnp.zeros(acc_ref.shape, F32)

    head_of_lane = lax.broadcasted_iota(I32, (N_HEADS, D_ATTN), 1) // HEAD_DIM
    own = head_of_lane == lax.broadcasted_iota(I32, (N_HEADS, D_ATTN), 0)
    qrows = jnp.where(own, q_ref[0], 0.0)
    qb = qrows.astype(BF16)
    flat = lambda pg: pg[...].reshape(D_ATTN, PAGE_SIZE).astype(BF16)
    s = jnp.concatenate([jnp.dot(qb, flat(pg), preferred_element_type=F32) for pg in kpages], axis=1)
    s = s + bp_ref[0]
    m_prev = m_ref[...]
    m_new = jnp.maximum(m_prev, jnp.max(s, axis=1, keepdims=True))
    alpha = jnp.exp(m_prev - m_new)
    pm = jnp.exp(s - m_new)
    l_ref[...] = alpha * l_ref[...] + jnp.sum(pm, axis=1, keepdims=True)
    m_ref[...] = m_new
    pmb = pm.astype(BF16)
    nt = (((1,), (1,)), ((), ()))
    pv = None
    for i, pg in enumerate(vpages):
        part = lax.dot_general(pmb[:, i * PAGE_SIZE:(i + 1) * PAGE_SIZE], flat(pg), nt, preferred_element_type=F32)
        pv = part if pv is None else pv + part
    acc_ref[...] = alpha * acc_ref[...] + pv

    @pl.when(g == pl.num_programs(1) - 1)
    def _():
        rnd = lambda x: x.astype(BF16).astype(F32)
        s_self = jnp.sum(rnd(qrows) * rnd(ks_ref[0]), axis=1, keepdims=True) + bs_ref[0]
        m_prev = m_ref[...]
        m_new = jnp.maximum(m_prev, s_self)
        alpha = jnp.exp(m_prev - m_new)
        p_self = jnp.exp(s_self - m_new)
        l = alpha * l_ref[...] + p_self
        acc = alpha * acc_ref[...] + rnd(p_self) * rnd(vs_ref[0])
        o_ref[0] = jnp.sum(jnp.where(own, acc / l, 0.0), axis=0, keepdims=True).astype(o_ref.dtype)


def _sample_attention(page_table_flat, q_s, bias_past, bias_self, k_s, v_s, cache_k_t, cache_v_t, layer, *,
                      n_pages, npg):
    nb = q_s.shape[0]
    ngrp = n_pages // npg
    kv_specs = _page_specs((None, None, N_HEADS, HEAD_DIM, PAGE_SIZE), layer, n_pages, npg)
    per_seq = lambda n: pl.BlockSpec((1, 1, n), lambda b, g, pt: (b, 0, 0))
    return pl.pallas_call(
        functools.partial(_sample_attn_kernel, npg=npg),
        out_shape=jax.ShapeDtypeStruct((nb, 1, D_ATTN), BF16),
        grid_spec=pltpu.PrefetchScalarGridSpec(
            num_scalar_prefetch=1,
            grid=(nb, ngrp),
            in_specs=[per_seq(D_ATTN),
                      pl.BlockSpec((1, 1, npg * PAGE_SIZE), lambda b, g, pt: (b, 0, g)),
                      per_seq(1), per_seq(D_ATTN), per_seq(D_ATTN)] + kv_specs + kv_specs,
            out_specs=per_seq(D_ATTN),
            scratch_shapes=[pltpu.VMEM((N_HEADS, 1), F32), pltpu.VMEM((N_HEADS, 1), F32),
                            pltpu.VMEM((N_HEADS, D_ATTN), F32)],
        ),
        compiler_params=_cparams(("parallel", "arbitrary")),
        name="sparse_attention_sample",
    )(page_table_flat, q_s, bias_past, bias_self, k_s, v_s, *([cache_k_t] * npg), *([cache_v_t] * npg))


def _pack_layer(l, w_in, b_in, conv_w, conv_b, rg_wa, rg_ba, rg_wi, rg_bi, rg_lam, w_oa, w_ob, w_out,
                ln1_g, ln1_b, w_ff1, w_ff2, ln2_g, ln2_b, w_pe, w_pg, b_pg):
    d_rnn = conv_w.shape[-1]
    w, b = w_in[l], b_in[l]
    n_idx = 4 * D_ATTN
    ki0, wi0, r0 = n_idx, n_idx + IDX_DIM, n_idx + IDX_DIM + IDX_HEADS

    def pad_cols(a, n):
        return jnp.pad(a, ((0, 0), (0, n - a.shape[1])))

    b2 = b[None, :]
    w_head = jnp.concatenate([w[:, :n_idx], pad_cols(w[:, ki0:wi0], LANES), pad_cols(w[:, wi0:r0], LANES)], axis=1)
    b_head = jnp.concatenate([b2[:, :n_idx], pad_cols(b2[:, ki0:wi0], LANES), pad_cols(b2[:, wi0:r0], LANES)], axis=1)
    vec = lambda a: a[l][None, :]
    return dict(
        w_head=w_head.astype(BF16), b_head=b_head, w_rest=w[:, r0:].astype(BF16), b_rest=b2[:, r0:],
        conv_w=conv_w[l], conv_b=vec(conv_b), rg_wa=rg_wa[l].astype(BF16), rg_ba=vec(rg_ba),
        rg_wi=rg_wi[l].astype(BF16), rg_bi=vec(rg_bi), rg_lam=vec(rg_lam),
        w_oa=w_oa[l].astype(BF16), w_ob=w_ob[l].astype(BF16), w_out=w_out[l].astype(BF16),
        ln1_g=vec(ln1_g), ln1_b=vec(ln1_b), w_ff1=w_ff1[l].astype(BF16), w_ff2=w_ff2[l].astype(BF16),
        ln2_g=vec(ln2_g), ln2_b=vec(ln2_b), w_pe=w_pe[l].astype(BF16), w_pg=w_pg[l].astype(BF16), b_pg=vec(b_pg),
        d_rnn=d_rnn)


def kernel(x_prompt, x_sample, p_prompt, p_sample, cache_k, cache_v, cache_kidx, state_h, state_conv, page_table, w_in, b_in, conv_w, conv_b, rg_wa, rg_ba, rg_wi, rg_bi, rg_lam, w_oa, w_ob, w_out, ln1_g, ln1_b, w_ff1, w_ff2, ln2_g, ln2_b, w_pe, w_pg, b_pg):
    batch, seq, d_model = x_prompt.shape
    nb, dec_seq, _ = x_sample.shape
    assert dec_seq == 1
    depth = w_in.shape[0]
    d_rnn = conv_w.shape[-1]
    n_pages = page_table.shape[1]
    past = n_pages * PAGE_SIZE
    alpha = (2.0 * depth) ** 0.25
    k_prompt_sel = min(TOPK_MAX, seq // 4)
    k_sample_sel = min(TOPK_MAX, (past + dec_seq) // 4)
    m = batch * seq
    tm = min(256, m)
    npg = math.gcd(n_pages, 16)
    npg_idx = math.gcd(n_pages, 32)
    pt_flat = page_table.reshape(-1).astype(I32)
    cache_k_t = jnp.transpose(cache_k, (0, 1, 3, 4, 2))
    cache_v_t = jnp.transpose(cache_v, (0, 1, 3, 4, 2))
    cache_kidx_t = jnp.transpose(cache_kidx, (0, 1, 3, 2))

    xp = x_prompt.reshape(m, d_model)
    xs = x_sample.reshape(nb, d_model)
    outs = [[] for _ in range(10)]
    for l in range(depth):
        lw = _pack_layer(l, w_in, b_in, conv_w, conv_b, rg_wa, rg_ba, rg_wi, rg_bi, rg_lam, w_oa, w_ob, w_out,
                         ln1_g, ln1_b, w_ff1, w_ff2, ln2_g, ln2_b, w_pe, w_pg, b_pg)
        rg = (lw["conv_w"], lw["conv_b"], lw["rg_wa"], lw["rg_ba"], lw["rg_wi"], lw["rg_bi"], lw["rg_lam"])

        (qt4, kx, vt, k32, v32, qpt, kp, ki32, wit, xr, gr, ga, gb) = _inproj(
            xp, lw["w_head"], lw["b_head"], lw["w_rest"], lw["b_rest"], prompt=True, tm=tm)
        bias = _index_topk(qpt, wit, kp, batch=batch, seq=seq, k=k_prompt_sel)
        o_attn = _attention(qt4, kx, vt, bias, batch=batch, seq=seq)
        y_rnn, h_last = _rglru(xr, gr, *rg, batch=batch, seq=seq)
        xp = _post(xp, o_attn, y_rnn, ga, gb, p_prompt[l].reshape(m, -1), lw, alpha=alpha, tm=tm,
                   name="post_prompt")
        outs[0].append(k32.reshape(batch, seq, N_HEADS, HEAD_DIM))
        outs[1].append(v32.reshape(batch, seq, N_HEADS, HEAD_DIM))
        outs[2].append(ki32.reshape(batch, seq, IDX_DIM))
        outs[3].append(h_last.reshape(batch, d_rnn))
        outs[4].append(xr.reshape(batch, seq, d_rnn)[:, seq - (CONV_W - 1):])

        (q_s, k_s, v_s, qp_s, ki_s, wi_s, ss_s, xr_s, gr_s, ga_s, gb_s) = _inproj(
            xs, lw["w_head"], lw["b_head"], lw["w_rest"], lw["b_rest"], prompt=False, tm=nb)
        scores_past = _sample_scores(pt_flat, jnp.swapaxes(qp_s, 0, 1), wi_s.reshape(nb, IDX_HEADS, 1),
                                     cache_kidx_t, l, n_pages=n_pages, npg=npg_idx)
        bias_s = _sample_select(scores_past.reshape(nb, past), ss_s, k=k_sample_sel)
        o_s = _sample_attention(
            pt_flat, q_s.reshape(nb, 1, D_ATTN), bias_s[:, :past].reshape(nb, 1, past),
            bias_s[:, past:past + 1].reshape(nb, 1, 1), k_s.reshape(nb, 1, D_ATTN), v_s.reshape(nb, 1, D_ATTN),
            cache_k_t, cache_v_t, l, n_pages=n_pages, npg=npg)
        y_s, h_s = _rglru_step(xr_s, gr_s, jnp.swapaxes(state_conv[l], 0, 1), state_h[l], *rg)
        xs = _post(xs, o_s.reshape(nb, D_ATTN), y_s, ga_s, gb_s, p_sample[l].reshape(nb, -1), lw, alpha=alpha,
                   tm=nb, name="post_sample")
        outs[5].append(k_s.reshape(nb, 1, N_HEADS, HEAD_DIM))
        outs[6].append(v_s.reshape(nb, 1, N_HEADS, HEAD_DIM))
        outs[7].append(ki_s.reshape(nb, 1, IDX_DIM))
        outs[8].append(h_s)
        outs[9].append(jnp.concatenate([state_conv[l][:, 1:], xr_s[:, None, :]], axis=1))

    return (xp.reshape(batch, seq, d_model), xs.reshape(nb, 1, d_model)) + tuple(jnp.stack(o) for o in outs)
```

```python
import functools
import math

import numpy as np
import jax
import jax.numpy as jnp
from jax import lax
from jax.experimental import pallas as pl
from jax.experimental.pallas import tpu as pltpu

F32 = jnp.float32
BF16 = jnp.bfloat16
I32 = jnp.int32

N_HEADS = 8
HEAD_DIM = 64
D_ATTN = N_HEADS * HEAD_DIM
IDX_HEADS = 8
IDX_DIM = 64
TOPK_MAX = 256
PAGE_SIZE = 128
RNN_BLOCKS = 8
CONV_W = 4
RG_C = 8.0
LN_EPS = 1e-5

LANES = 128
VMEM_LIMIT = 56 * 1024 * 1024
NEG_BIAS = -1e30
QUERY_BLOCK = 256
LOG2E = math.log2(math.e)
F32_LOWEST = float(np.finfo(np.float32).min)

_SEG = {}
_off = 0
for _name, _n in (("q", D_ATTN), ("k", D_ATTN), ("v", D_ATTN), ("qi", IDX_HEADS * IDX_DIM),
                  ("ki", LANES), ("wi", LANES)):
    _SEG[_name] = (_off, _off + _n)
    _off += _n


def _cparams(sem):
    return pltpu.CompilerParams(dimension_semantics=sem, vmem_limit_bytes=VMEM_LIMIT)


def _resident(shape):
    nd = len(shape)
    return pl.BlockSpec(shape, lambda *_: (0,) * nd, pipeline_mode=pl.Buffered(1))


def _sigmoid(x):
    return 1.0 / (1.0 + jnp.exp(-x))


def _gelu_tanh(x):
    c = math.sqrt(2.0 / math.pi)
    return 0.5 * x * (1.0 + jnp.tanh(c * (x + 0.044715 * (x * x * x))))


def _expm1(x):
    u = jnp.exp(x)
    um1 = u - 1.0
    return jnp.where(u == 1.0, x, jnp.where(x < -0.5, um1, um1 * x / jnp.log(u)))


def _layer_norm(x, g, b):
    mu = jnp.mean(x, axis=-1, keepdims=True)
    xc = x - mu
    var = jnp.mean(xc * xc, axis=-1, keepdims=True)
    return xc * lax.rsqrt(var + LN_EPS) * g + b


def _split_hi_lo(x):
    hi = x.astype(BF16)
    lo = (x - hi.astype(F32)).astype(BF16)
    return hi, lo


def _swap_halves(t):
    return jnp.concatenate([t[:, HEAD_DIM:], t[:, :HEAD_DIM]], axis=1)


def _inproj_kernel(x_ref, w_ref, b_ref, wr_ref, br_ref, *out_refs, prompt, d_rnn):
    if prompt:
        (qt_ref, kx_ref, vt_ref, k_ref, v_ref, qpt_ref, kp_ref, ki_ref, wit_ref,
         xr_ref, gr_ref, ga_ref, gb_ref) = out_refs
    else:
        (q_ref, k_ref, v_ref, qp_ref, ki_ref, wi_ref, ss_ref,
         xr_ref, gr_ref, ga_ref, gb_ref) = out_refs
    xb = x_ref[...].astype(BF16)
    tm = xb.shape[0]

    def seg(name):
        a, b = _SEG[name]
        return jnp.dot(xb, w_ref[:, a:b], preferred_element_type=F32) + b_ref[:, a:b]

    lane = lax.broadcasted_iota(I32, (tm, LANES), 1)
    low = lane < HEAD_DIM

    zq = seg("q") * (HEAD_DIM ** -0.5 * (LOG2E if prompt else 1.0))
    if prompt:
        for p in range(N_HEADS // 2):
            qt_ref[p] = zq[:, p * LANES:(p + 1) * LANES].T.astype(BF16)
    else:
        q_ref[...] = zq

    zk = seg("k")
    k_ref[...] = zk
    zv = seg("v")
    v_ref[...] = zv
    if prompt:
        zkb = zk.astype(BF16)
        zero_b = jnp.zeros((tm, LANES), BF16)
        for p in range(N_HEADS // 2):
            kt = zkb[:, p * LANES:(p + 1) * LANES]
            kx_ref[2 * p] = jnp.where(low, kt, zero_b)
            kx_ref[2 * p + 1] = jnp.where(low, zero_b, kt)
            vt_ref[p * LANES:(p + 1) * LANES, :] = zv[:, p * LANES:(p + 1) * LANES].T.astype(BF16)

    zqi = seg("qi") * (IDX_DIM ** -0.5)
    zki = seg("ki")
    ki_ref[...] = zki[:, :IDX_DIM]
    zwi = seg("wi") * (IDX_HEADS ** -0.5)
    kdup = jnp.where(low, zki, _swap_halves(zki))
    if prompt:
        wit_ref[...] = zwi.T[:IDX_HEADS]
        khi, klo = _split_hi_lo(kdup)
        khl = jnp.where(low, khi, klo)
        kp_ref[:, 0:LANES] = khl
        kp_ref[:, LANES:2 * LANES] = khl
    else:
        wi_ref[...] = zwi[:, :IDX_HEADS]
    qblk = min(QUERY_BLOCK, tm)
    n_qblk = tm // qblk if prompt else 0
    for p in range(IDX_HEADS // 2):
        t = zqi[:, p * LANES:(p + 1) * LANES]
        sw = _swap_halves(t)
        for e in range(2):
            h = 2 * p + e
            dup = jnp.where(low, t, sw) if e == 0 else jnp.where(low, sw, t)
            if prompt:
                hi, lo = _split_hi_lo(dup.T)
                for r in range(n_qblk):
                    c0 = (r * IDX_HEADS + h) * qblk
                    qpt_ref[0:LANES, c0:c0 + qblk] = hi[:, r * qblk:(r + 1) * qblk]
                    qpt_ref[LANES:2 * LANES, c0:c0 + qblk] = lo[:, r * qblk:(r + 1) * qblk]
            else:
                hi, lo = _split_hi_lo(dup)
                qp_ref[h, :, 0:LANES] = hi
                qp_ref[h, :, LANES:2 * LANES] = lo
    if not prompt:
        acc = jnp.zeros((tm, 1), F32)
        for p in range(IDX_HEADS // 2):
            prod = zqi[:, p * LANES:(p + 1) * LANES] * kdup
            for e in range(2):
                h = 2 * p + e
                keep = low if e == 0 else jnp.logical_not(low)
                s_h = jnp.sum(jnp.where(keep, prod, 0.0), axis=1, keepdims=True)
                acc = acc + zwi[:, h:h + 1] * jnp.maximum(s_h, 0.0)
        ss_ref[...] = acc

    for i, ref in enumerate((xr_ref, gr_ref, ga_ref, gb_ref)):
        a, b = i * d_rnn, (i + 1) * d_rnn
        ref[...] = jnp.dot(xb, wr_ref[:, a:b], preferred_element_type=F32) + br_ref[:, a:b]


def _inproj(x2d, w_head, b_head, w_rest, b_rest, *, prompt, tm):
    m, d = x2d.shape
    d_rnn = w_rest.shape[1] // 4
    sds = jax.ShapeDtypeStruct
    row = lambda n: pl.BlockSpec((tm, n), lambda i: (i, 0))
    col = lambda n: pl.BlockSpec((n, tm), lambda i: (0, i))
    wide = [sds((m, d_rnn), F32)] * 4
    wide_specs = [row(d_rnn)] * 4
    if prompt:
        qpt_cols = IDX_HEADS * tm
        out_shape = [sds((N_HEADS // 2, LANES, m), BF16), sds((N_HEADS, m, LANES), BF16),
                     sds((D_ATTN, m), BF16), sds((m, D_ATTN), F32), sds((m, D_ATTN), F32),
                     sds((2 * LANES, IDX_HEADS * m), BF16), sds((m, 2 * LANES), BF16),
                     sds((m, IDX_DIM), F32), sds((IDX_HEADS, m), F32)] + wide
        out_specs = [pl.BlockSpec((N_HEADS // 2, LANES, tm), lambda i: (0, 0, i)),
                     pl.BlockSpec((N_HEADS, tm, LANES), lambda i: (0, i, 0)),
                     col(D_ATTN), row(D_ATTN), row(D_ATTN),
                     pl.BlockSpec((2 * LANES, qpt_cols), lambda i: (0, i)), row(2 * LANES),
                     row(IDX_DIM), col(IDX_HEADS)] + wide_specs
    else:
        out_shape = [sds((m, D_ATTN), F32), sds((m, D_ATTN), F32), sds((m, D_ATTN), F32),
                     sds((IDX_HEADS, m, 2 * LANES), BF16), sds((m, IDX_DIM), F32), sds((m, IDX_HEADS), F32),
                     sds((m, 1), F32)] + wide
        out_specs = [row(D_ATTN), row(D_ATTN), row(D_ATTN),
                     pl.BlockSpec((IDX_HEADS, tm, 2 * LANES), lambda i: (0, i, 0)),
                     row(IDX_DIM), row(IDX_HEADS), row(1)] + wide_specs
    return pl.pallas_call(
        functools.partial(_inproj_kernel, prompt=prompt, d_rnn=d_rnn),
        out_shape=out_shape,
        grid=(m // tm,),
        in_specs=[row(d), _resident(w_head.shape), _resident(b_head.shape),
                  _resident(w_rest.shape), _resident(b_rest.shape)],
        out_specs=out_specs,
        compiler_params=_cparams(("parallel",)),
        name="inproj_prompt" if prompt else "inproj_sample",
    )(x2d, w_head, b_head, w_rest, b_rest)


def _ukey_to_f32(u):
    sk = u ^ jnp.int32(-2 ** 31)
    bits = jnp.where(sk >= 0, sk, sk ^ jnp.int32(2 ** 31 - 1))
    return lax.bitcast_convert_type(bits, F32)


def _select_to_bias(s_ref, o_ref, nch, n_adm, *, key_axis, width, chunk, k, nkeys, bounds=None):
    kf = float(k)
    sub = 64 if key_axis == 0 else LANES
    tiles = chunk // sub
    tshape = (sub, width) if key_axis == 0 else (width, sub)
    stat_shape = (1, width) if key_axis == 0 else (width, 1)

    def tile(ref, o):
        return ref.at[pl.ds(o, sub), :] if key_axis == 0 else ref.at[:, pl.ds(o, sub)]

    def count(pred):
        def body(c, acc):
            off = pl.multiple_of(c * chunk, chunk)
            for u in range(tiles):
                o = off + u * sub
                acc = acc + pred(tile(s_ref, o)[...], o)
            return acc
        acc = lax.fori_loop(0, nch, body, jnp.zeros(tshape, F32))
        return jnp.sum(acc, axis=key_axis, keepdims=True)

    def wide(v):
        return jnp.broadcast_to(v, tshape)

    search = n_adm > k
    done0 = jnp.where(search, 0.0, 1.0).astype(F32)

    def count_ge(v):
        v_w = wide(v)
        return count(lambda t, o: jnp.where(t >= v_w, 1.0, 0.0))

    def all_done(done):
        return jnp.sum(done) >= float(width)

    if bounds is None:
        def bit_body(st):
            i, cur, done = st
            trial = cur | lax.shift_left(jnp.int32(1), 31 - i)
            cnt = count_ge(_ukey_to_f32(trial))
            cur = jnp.where(jnp.logical_and(done == 0.0, cnt >= kf), trial, cur)
            return i + 1, cur, jnp.where(cnt == kf, 1.0, done)

        _, cur, _ = lax.while_loop(lambda st: jnp.logical_and(st[0] < 32, jnp.logical_not(all_done(st[2]))),
                                   bit_body, (jnp.int32(0), jnp.zeros(stat_shape, I32), done0))
        kth = _ukey_to_f32(cur)
    else:
        lo, hi = bounds
        hi = hi + 0.0
        hb = lax.bitcast_convert_type(hi, I32)
        hi = lax.bitcast_convert_type(jnp.where(hi >= 0.0, hb + 1, hb - 1), F32)
        zero = jnp.zeros(stat_shape, F32)
        n_ge0 = count_ge(zero)
        n_gt0 = count(lambda t, o: jnp.where(t > 0.0, 1.0, 0.0))
        at_zero = jnp.logical_and(n_ge0 >= kf, n_gt0 < kf)
        lo = jnp.where(n_ge0 >= kf, jnp.maximum(lo, 0.0), lo)
        hi = jnp.where(n_ge0 >= kf, hi, jnp.minimum(hi, 0.0))
        done = jnp.where(at_zero, 1.0, done0)
        kth0 = jnp.where(at_zero, 0.0, lo)
        ckth0 = jnp.where(at_zero, n_ge0, jnp.inf)

        def bis_body(st):
            i, lo, hi, kth, ckth, done = st
            mid = 0.5 * lo + 0.5 * hi
            stuck = jnp.logical_or(mid <= lo, mid >= hi)
            cnt = count_ge(mid)
            open_ = jnp.logical_and(done == 0.0, jnp.logical_not(stuck))
            up = jnp.logical_and(open_, cnt >= kf)
            down = jnp.logical_and(open_, cnt < kf)
            done = jnp.where(jnp.logical_or(stuck, cnt == kf), 1.0, done)
            return (i + 1, jnp.where(up, mid, lo), jnp.where(down, mid, hi), jnp.where(up, mid, kth),
                    jnp.where(up, cnt, ckth), done)

        st = lax.fori_loop(0, 16, lambda _, st: bis_body(st), (jnp.int32(0), lo, hi, kth0, ckth0, done))
        st = lax.while_loop(lambda st: jnp.logical_and(st[0] < 320, jnp.logical_not(all_done(st[5]))), bis_body, st)
        kth, cnt_ge = st[3], st[4]
    thr = jnp.where(search, kth, F32_LOWEST)
    thr_w = wide(thr)
    if bounds is None:
        cnt_ge = count_ge(thr)
    tie = jnp.logical_and(search, cnt_ge > kf)
    n_tie = jnp.sum(jnp.where(tie, 1.0, 0.0))

    def write(bias_fn):
        def body(c, carry):
            off = pl.multiple_of(c * chunk, chunk)
            for u in range(tiles):
                o = off + u * sub
                tile(o_ref, o)[...] = bias_fn(tile(s_ref, o)[...], o).astype(o_ref.dtype)
            return carry
        lax.fori_loop(0, nch, body, 0)

    @pl.when(n_tie == 0.0)
    def _():
        write(lambda t, o: jnp.where(t >= thr_w, 0.0, NEG_BIAS))

    @pl.when(n_tie > 0.0)
    def _():
        cnt_gt = count(lambda t, o: jnp.where(t > thr_w, 1.0, 0.0))
        need = jnp.where(search, kf - cnt_gt, float(2 * nkeys))
        tc = chunk if key_axis == 0 else LANES
        ri = lax.broadcasted_iota(I32, (tc, tc), 0)
        ci = lax.broadcasted_iota(I32, (tc, tc), 1)
        tri = jnp.where((ri >= ci) if key_axis == 0 else (ri <= ci), 1.0, 0.0).astype(BF16)

        def body(c, seen):
            off = pl.multiple_of(c * tc, tc)
            src = s_ref.at[pl.ds(off, tc), :] if key_axis == 0 else s_ref.at[:, pl.ds(off, tc)]
            dst = o_ref.at[pl.ds(off, tc), :] if key_axis == 0 else o_ref.at[:, pl.ds(off, tc)]
            t = src[...]
            eq = t == thr
            eqb = jnp.where(eq, 1.0, 0.0).astype(BF16)
            if key_axis == 0:
                rank = jnp.dot(tri, eqb, preferred_element_type=F32) + seen
                last = rank[tc - 1:tc, :]
            else:
                rank = jnp.dot(eqb, tri, preferred_element_type=F32) + seen
                last = seen + jnp.sum(jnp.where(eq, 1.0, 0.0), axis=1, keepdims=True)
            dst[...] = jnp.where(eq, jnp.where(rank <= need, 0.0, NEG_BIAS),
                                 jnp.where(t > thr, 0.0, NEG_BIAS)).astype(o_ref.dtype)
            return last

        lax.fori_loop(0, nch * (chunk // tc), body, jnp.zeros(stat_shape, F32))


def _index_topk_kernel(qpt_ref, wit_ref, kp_ref, o_ref, s_ref, *, seq, cols, schunk, chunk, k):
    j = pl.program_id(1)
    nkeys = j * cols + cols
    nch = (nkeys + chunk - 1) // chunk
    nsc = (nkeys + schunk - 1) // schunk
    w = wit_ref[...]
    qpos = j * cols + lax.broadcasted_iota(I32, (schunk, cols), 1)
    krow = lax.broadcasted_iota(I32, (schunk, cols), 0)

    def score_body(c, carry):
        off = pl.multiple_of(c * schunk, schunk)
        kc = kp_ref[pl.ds(off, schunk), :]
        acc = None
        for pr in range(IDX_HEADS // 2):
            st = jnp.dot(kc, qpt_ref[:, pr * 2 * cols:(pr + 1) * 2 * cols], preferred_element_type=F32)
            for e in range(2):
                h = 2 * pr + e
                term = w[h:h + 1, :] * jnp.maximum(st[:, e * cols:(e + 1) * cols], 0.0)
                acc = term if acc is None else acc + term
        s_ref[pl.ds(off, schunk), :] = jnp.where(krow + off <= qpos, acc, -jnp.inf)
        return carry

    lax.fori_loop(0, nsc, score_body, 0)
    n_adm = j * cols + lax.broadcasted_iota(I32, (1, cols), 1) + 1
    bounds = None
    if cols >= k:
        def fold(c, mx):
            return jnp.maximum(mx, s_ref[pl.ds(pl.multiple_of(c * cols, cols), cols), :])
        class_max = lax.fori_loop(0, nkeys // cols, fold, jnp.full((cols, cols), -jnp.inf, F32))
        bounds = (jnp.min(class_max, axis=0, keepdims=True), jnp.max(class_max, axis=0, keepdims=True))
    _select_to_bias(s_ref, o_ref, nch, n_adm, key_axis=0, width=cols, chunk=chunk, k=k, nkeys=seq, bounds=bounds)

    def fill(c, carry):
        off = pl.multiple_of(c * chunk, chunk)
        o_ref[pl.ds(off, chunk), :] = jnp.full((chunk, cols), NEG_BIAS, o_ref.dtype)
        return carry

    lax.fori_loop(nch, seq // chunk, fill, 0)


def _index_topk(qpt, wit, kp, *, batch, seq, k):
    cols = min(QUERY_BLOCK, seq)
    chunk = min(512, seq)
    schunk = min(1024, seq)
    nqb = seq // cols
    return pl.pallas_call(
        functools.partial(_index_topk_kernel, seq=seq, cols=cols, schunk=schunk, chunk=chunk, k=k),
        out_shape=jax.ShapeDtypeStruct((batch, seq, seq), BF16),
        grid=(batch, nqb),
        in_specs=[pl.BlockSpec((2 * LANES, IDX_HEADS * cols), lambda b, j: (0, b * nqb + j)),
                  pl.BlockSpec((IDX_HEADS, cols), lambda b, j: (0, b * nqb + j)),
                  pl.BlockSpec((seq, 2 * LANES), lambda b, j: (b, 0))],
        out_specs=pl.BlockSpec((None, seq, cols), lambda b, j: (b, 0, j)),
        scratch_shapes=[pltpu.VMEM((seq, cols), F32)],
        compiler_params=_cparams(("parallel", "arbitrary")),
        name="index_topk_prompt",
    )(qpt, wit, kp)


def _attn_kernel(qt_tab, kt_tab, qt_ref, kx_ref, vt_ref, b_ref, o_ref, m_ref, l_ref, acc_ref, *, tq, tk):
    p = pl.program_id(1)
    kt = kt_tab[p]
    last = (qt_tab[p] * tq + tq - 1) // tk

    @pl.when(kt == 0)
    def _():
        m_ref[...] = jnp.full(m_ref.shape, NEG_BIAS, F32)
        l_ref[...] = jnp.zeros(l_ref.shape, F32)
        acc_ref[...] = jnp.zeros(acc_ref.shape, F32)

    def over_keys(op, x):
        slab = max(x.shape[0] // 2, 8)
        parts = [x[i:i + slab] for i in range(0, x.shape[0], slab)]
        while len(parts) > 1:
            parts = [op(parts[i], parts[i + 1]) for i in range(0, len(parts), 2)]
        return (jnp.max if op is jnp.maximum else jnp.sum)(parts[0], axis=0, keepdims=True)

    bias = b_ref[...].astype(F32)
    scores = [jnp.dot(kx_ref[h], qt_ref[h // 2], preferred_element_type=F32) + bias for h in range(N_HEADS)]
    for h, s in enumerate(scores):
        rows = slice(h * HEAD_DIM, (h + 1) * HEAD_DIM)
        pms, alphas = [], []
        for c0 in range(0, tq, LANES):
            cs = slice(c0, c0 + LANES)
            sq = s[:, cs]
            m_prev = m_ref[h:h + 1, cs]
            m_new = jnp.maximum(m_prev, over_keys(jnp.maximum, sq))
            alpha = jnp.exp2(m_prev - m_new)
            pm = jnp.exp2(sq - m_new)
            l_ref[h:h + 1, cs] = alpha * l_ref[h:h + 1, cs] + over_keys(jnp.add, pm)
            m_ref[h:h + 1, cs] = m_new
            pms.append(pm.astype(BF16))
            alphas.append(alpha)
        acc_ref[rows, :] = (jnp.concatenate(alphas, axis=1) * acc_ref[rows, :]
                            + jnp.dot(vt_ref[rows, :], jnp.concatenate(pms, axis=1), preferred_element_type=F32))

    @pl.when(kt == last)
    def _():
        for h in range(N_HEADS):
            rows = slice(h * HEAD_DIM, (h + 1) * HEAD_DIM)
            acc_ref[rows, :] = acc_ref[rows, :] / l_ref[h:h + 1, :]
        o_ref[...] = acc_ref[...].T.astype(o_ref.dtype)


def _attention(qt4, kx, vt, bias, *, batch, seq):
    tq = min(512, seq)
    tk = min(512, seq)
    nq, nk = seq // tq, seq // tk
    qt, kt = [], []
    for i in range(nq):
        for c in range((i * tq + tq - 1) // tk + 1):
            qt.append(i)
            kt.append(c)
    npairs = len(qt)
    m = batch * seq
    return pl.pallas_call(
        functools.partial(_attn_kernel, tq=tq, tk=tk),
        out_shape=jax.ShapeDtypeStruct((m, D_ATTN), BF16),
        grid_spec=pltpu.PrefetchScalarGridSpec(
            num_scalar_prefetch=2,
            grid=(batch, npairs),
            in_specs=[
                pl.BlockSpec((N_HEADS // 2, LANES, tq), lambda b, p, qt, kt: (0, 0, b * nq + qt[p])),
                pl.BlockSpec((N_HEADS, tk, LANES), lambda b, p, qt, kt: (0, b * nk + kt[p], 0)),
                pl.BlockSpec((D_ATTN, tk), lambda b, p, qt, kt: (0, b * nk + kt[p])),
                pl.BlockSpec((None, tk, tq), lambda b, p, qt, kt: (b, kt[p], qt[p])),
            ],
            out_specs=pl.BlockSpec((tq, D_ATTN), lambda b, p, qt, kt: (b * nq + qt[p], 0)),
            scratch_shapes=[pltpu.VMEM((N_HEADS, tq), F32), pltpu.VMEM((N_HEADS, tq), F32),
                            pltpu.VMEM((D_ATTN, tq), F32)],
        ),
        compiler_params=_cparams(("parallel", "arbitrary")),
        name="sparse_attention_prompt",
    )(jnp.asarray(qt, I32), jnp.asarray(kt, I32), qt4, kx, vt, bias)


def _rglru_gates(xc, wa_ref, ba_ref, wg_ref, bg_ref, lam_ref, store):
    xcb = xc.astype(BF16)
    lam = lam_ref[...]
    nlam = -lam
    softplus = jnp.maximum(nlam, 0.0) + jnp.log1p(jnp.exp(-jnp.abs(nlam)))
    bw = xc.shape[1] // RNN_BLOCKS
    for n in range(RNN_BLOCKS):
        sl = slice(n * bw, (n + 1) * bw)
        r = _sigmoid(jnp.dot(xcb[:, sl], wa_ref[n], preferred_element_type=F32) + ba_ref[:, sl])
        ig = _sigmoid(jnp.dot(xcb[:, sl], wg_ref[n], preferred_element_type=F32) + bg_ref[:, sl])
        log_a = (-RG_C) * r * softplus[:, sl]
        a = jnp.exp(log_a)
        bx = jnp.sqrt(-_expm1(2.0 * log_a)) * (ig * xc[:, sl])
        store(sl, a, bx)


def _rglru_kernel(xr_ref, gr_ref, cw_ref, cb_ref, wa_ref, ba_ref, wg_ref, bg_ref, lam_ref,
                  y_ref, hl_ref, xbuf, a_s, b_s, h_s, *, tc):
    c = pl.program_id(1)
    pad = 8

    @pl.when(c == 0)
    def _():
        xbuf[0:pad] = jnp.zeros((pad, xbuf.shape[1]), F32)
        h_s[...] = jnp.zeros(h_s.shape, F32)

    x = xr_ref[...]
    xbuf[pad:pad + tc] = x
    cw = cw_ref[...]
    xc = cb_ref[...] + cw[CONV_W - 1:CONV_W] * x
    for jj in range(1, CONV_W):
        xc = xc + cw[CONV_W - 1 - jj:CONV_W - jj] * xbuf[pad - jj:pad - jj + tc]
    xbuf[0:pad] = x[tc - pad:tc]

    def store(sl, a, bx):
        a_s[:, sl] = a
        b_s[:, sl] = bx

    _rglru_gates(xc, wa_ref, ba_ref, wg_ref, bg_ref, lam_ref, store)

    def step(t, h):
        h = a_s[pl.ds(t, 1), :] * h + b_s[pl.ds(t, 1), :]
        b_s[pl.ds(t, 1), :] = h
        return h

    h = lax.fori_loop(0, tc, step, h_s[...], unroll=8)
    h_s[...] = h
    hl_ref[0] = h
    y_ref[...] = (b_s[...] * _gelu_tanh(gr_ref[...])).astype(y_ref.dtype)


def _rglru(xr, gr, cw, cb, wa, ba, wg, bg, lam, *, batch, seq):
    tc = min(512, seq)
    nc = seq // tc
    m, c = xr.shape
    row = pl.BlockSpec((tc, c), lambda b, i: (b * nc + i, 0))
    vec = pl.BlockSpec((1, c), lambda b, i: (0, 0))
    blk = pl.BlockSpec(wa.shape, lambda b, i: (0, 0, 0))
    return pl.pallas_call(
        functools.partial(_rglru_kernel, tc=tc),
        out_shape=[jax.ShapeDtypeStruct((m, c), BF16), jax.ShapeDtypeStruct((batch, 1, c), F32)],
        grid=(batch, nc),
        in_specs=[row, row, pl.BlockSpec((CONV_W, c), lambda b, i: (0, 0)), vec, blk, vec, blk, vec, vec],
        out_specs=[row, pl.BlockSpec((1, 1, c), lambda b, i: (b, 0, 0))],
        scratch_shapes=[pltpu.VMEM((tc + 8, c), F32), pltpu.VMEM((tc, c), F32), pltpu.VMEM((tc, c), F32),
                        pltpu.VMEM((1, c), F32)],
        compiler_params=_cparams(("parallel", "arbitrary")),
        name="rglru_prompt",
    )(xr, gr, cw, cb, wa, ba, wg, bg, lam)


def _rglru_step_kernel(xr_ref, gr_ref, sc_ref, h0_ref, cw_ref, cb_ref, wa_ref, ba_ref, wg_ref, bg_ref, lam_ref,
                       y_ref, h_ref):
    x = xr_ref[...]
    cw = cw_ref[...]
    xc = cb_ref[...] + cw[CONV_W - 1:CONV_W] * x
    for jj in range(CONV_W - 1):
        xc = xc + cw[jj:jj + 1] * sc_ref[jj]
    h0 = h0_ref[...]

    def store(sl, a, bx):
        h_ref[:, sl] = a * h0[:, sl] + bx

    _rglru_gates(xc, wa_ref, ba_ref, wg_ref, bg_ref, lam_ref, store)
    y_ref[...] = (h_ref[...] * _gelu_tanh(gr_ref[...])).astype(y_ref.dtype)


def _rglru_step(xr, gr, sc, h0, cw, cb, wa, ba, wg, bg, lam):
    m, c = xr.shape
    return pl.pallas_call(
        _rglru_step_kernel,
        out_shape=[jax.ShapeDtypeStruct((m, c), BF16), jax.ShapeDtypeStruct((m, c), F32)],
        compiler_params=pltpu.CompilerParams(vmem_limit_bytes=VMEM_LIMIT),
        name="rglru_sample",
    )(xr, gr, sc, h0, cw, cb, wa, ba, wg, bg, lam)


def _post_kernel(x_ref, o_ref, y_ref, ga_ref, gb_ref, pe_ref,
                 woa_ref, wob_ref, wout_ref, g1_ref, b1_ref, wf1_ref, wf2_ref, g2_ref, b2_ref,
                 wpe_ref, wpg_ref, bpg_ref, out_ref, *, alpha, ff_chunk):
    dot = functools.partial(jnp.dot, preferred_element_type=F32)
    merged = (_sigmoid(ga_ref[...]) * dot(o_ref[...], woa_ref[...])
              + _sigmoid(gb_ref[...]) * dot(y_ref[...], wob_ref[...]))
    x1 = _layer_norm(alpha * x_ref[...] + dot(merged.astype(BF16), wout_ref[...]), g1_ref[...], b1_ref[...])
    x1b = x1.astype(BF16)
    d_ff = wf1_ref.shape[1]
    ff = None
    for c0 in range(0, d_ff, ff_chunk):
        hcol = jnp.maximum(dot(x1b, wf1_ref[:, c0:c0 + ff_chunk]), 0.0)
        part = dot((hcol * hcol).astype(BF16), wf2_ref[c0:c0 + ff_chunk, :])
        ff = part if ff is None else ff + part
    x2 = _layer_norm(alpha * x1 + ff, g2_ref[...], b2_ref[...])
    gate = _sigmoid(dot(x2.astype(BF16), wpg_ref[...]) + bpg_ref[...])
    out_ref[...] = x2 + gate * dot(pe_ref[...].astype(BF16), wpe_ref[...])


def _post(x, o, y, ga, gb, pe, lw, *, alpha, tm, name):
    m, d = x.shape
    row = lambda n: pl.BlockSpec((tm, n), lambda i: (i, 0))
    weights = (lw["w_oa"], lw["w_ob"], lw["w_out"], lw["ln1_g"], lw["ln1_b"], lw["w_ff1"], lw["w_ff2"],
               lw["ln2_g"], lw["ln2_b"], lw["w_pe"], lw["w_pg"], lw["b_pg"])
    return pl.pallas_call(
        functools.partial(_post_kernel, alpha=alpha, ff_chunk=min(1024, lw["w_ff1"].shape[1])),
        out_shape=jax.ShapeDtypeStruct((m, d), F32),
        grid=(m // tm,),
        in_specs=[row(d), row(o.shape[1]), row(y.shape[1]), row(d), row(d), row(pe.shape[1])]
        + [_resident(w.shape) for w in weights],
        out_specs=row(d),
        compiler_params=_cparams(("parallel",)),
        name=name,
    )(x, o, y, ga, gb, pe, *weights)


def _page_specs(block, layer, n_pages, npg):
    nd = len(block)

    def index_map(b, g, pt, i):
        return (layer, pt[b * n_pages + g * npg + i]) + (0,) * (nd - 2)

    return [pl.BlockSpec(block, functools.partial(index_map, i=i)) for i in range(npg)]


def _sample_scores_kernel(pt_ref, qp_ref, w_ref, *refs, npg):
    pages, o_ref = refs[:npg], refs[npg]
    qp = qp_ref[0]
    q2 = jnp.concatenate([qp[:, 0:IDX_DIM], qp[:, 2 * IDX_DIM:3 * IDX_DIM]], axis=0)
    w = w_ref[0]
    for i, pg in enumerate(pages):
        kh, kl = _split_hi_lo(pg[...])
        ab = jnp.dot(q2, kh, preferred_element_type=F32) + jnp.dot(q2, kl, preferred_element_type=F32)
        s = ab[0:IDX_HEADS] + ab[IDX_HEADS:]
        o_ref[0, :, i * PAGE_SIZE:(i + 1) * PAGE_SIZE] = jnp.sum(w * jnp.maximum(s, 0.0), axis=0, keepdims=True)


def _sample_scores(page_table_flat, qp_s, wi_s, cache_kidx_t, layer, *, n_pages, npg):
    nb = qp_s.shape[0]
    ngrp = n_pages // npg
    return pl.pallas_call(
        functools.partial(_sample_scores_kernel, npg=npg),
        out_shape=jax.ShapeDtypeStruct((nb, 1, n_pages * PAGE_SIZE), F32),
        grid_spec=pltpu.PrefetchScalarGridSpec(
            num_scalar_prefetch=1,
            grid=(nb, ngrp),
            in_specs=[pl.BlockSpec((1, IDX_HEADS, 2 * LANES), lambda b, g, pt: (b, 0, 0)),
                      pl.BlockSpec((1, IDX_HEADS, 1), lambda b, g, pt: (b, 0, 0))]
            + _page_specs((None, None, IDX_DIM, PAGE_SIZE), layer, n_pages, npg),
            out_specs=pl.BlockSpec((1, 1, npg * PAGE_SIZE), lambda b, g, pt: (b, 0, g)),
        ),
        compiler_params=_cparams(("parallel", "arbitrary")),
        name="index_scores_sample",
    )(page_table_flat, qp_s, wi_s, *([cache_kidx_t] * npg))


def _sample_select_kernel(sp_ref, ss_ref, o_ref, s_ref, *, past, k, chunk):
    rows = sp_ref.shape[0]
    ncols = past + LANES
    s_ref[:, 0:past] = sp_ref[...]
    lane = lax.broadcasted_iota(I32, (rows, LANES), 1)
    s_ref[:, past:ncols] = jnp.where(lane == 0, ss_ref[...], -jnp.inf)
    n_adm = jnp.full((rows, 1), past + 1, I32)
    _select_to_bias(s_ref, o_ref, ncols // chunk, n_adm, key_axis=1, width=rows, chunk=chunk, k=k, nkeys=ncols)


def _sample_select(scores_past, score_self, *, k):
    nb, past = scores_past.shape
    ncols = past + LANES
    chunk = LANES * math.gcd(ncols // LANES, 5)
    return pl.pallas_call(
        functools.partial(_sample_select_kernel, past=past, k=k, chunk=chunk),
        out_shape=jax.ShapeDtypeStruct((nb, ncols), F32),
        scratch_shapes=[pltpu.VMEM((nb, ncols), F32)],
        compiler_params=pltpu.CompilerParams(vmem_limit_bytes=VMEM_LIMIT),
        name="index_select_sample",
    )(scores_past, score_self)


def _sample_attn_kernel(pt_ref, q_ref, bp_ref, bs_ref, ks_ref, vs_ref, *refs, npg):
    kpages, vpages = refs[:npg], refs[npg:2 * npg]
    o_ref, m_ref, l_ref, acc_ref = refs[2 * npg:]
    g = pl.program_id(1)

    @pl.when(g == 0)
    def _():
        m_ref[...] = jnp.full(m_ref.shape, NEG_BIAS, F32)
        l_ref[...] = jnp.zeros(l_ref.shape, F32)
        acc_ref[...] = jnp.zeros(acc_ref.shape, F32)

    head_of_lane = lax.broadcasted_iota(I32, (N_HEADS, D_ATTN), 1) // HEAD_DIM
    own = head_of_lane == lax.broadcasted_iota(I32, (N_HEADS, D_ATTN), 0)
    qrows = jnp.where(own, q_ref[0], 0.0)
    qb = qrows.astype(BF16)
    flat = lambda pg: pg[...].reshape(D_ATTN, PAGE_SIZE).astype(BF16)
    s = jnp.concatenate([jnp.dot(qb, flat(pg), preferred_element_type=F32) for pg in kpages], axis=1)
    s = s + bp_ref[0]
    m_prev = m_ref[...]
    m_new = jnp.maximum(m_prev, jnp.max(s, axis=1, keepdims=True))
    alpha = jnp.exp(m_prev - m_new)
    pm = jnp.exp(s - m_new)
    l_ref[...] = alpha * l_ref[...] + jnp.sum(pm, axis=1, keepdims=True)
    m_ref[...] = m_new
    pmb = pm.astype(BF16)
    nt = (((1,), (1,)), ((), ()))
    pv = None
    for i, pg in enumerate(vpages):
        part = lax.dot_general(pmb[:, i * PAGE_SIZE:(i + 1) * PAGE_SIZE], flat(pg), nt, preferred_element_type=F32)
        pv = part if pv is None else pv + part
    acc_ref[...] = alpha * acc_ref[...] + pv

    @pl.when(g == pl.num_programs(1) - 1)
    def _():
        rnd = lambda x: x.astype(BF16).astype(F32)
        s_self = jnp.sum(rnd(qrows) * rnd(ks_ref[0]), axis=1, keepdims=True) + bs_ref[0]
        m_prev = m_ref[...]
        m_new = jnp.maximum(m_prev, s_self)
        alpha = jnp.exp(m_prev - m_new)
        p_self = jnp.exp(s_self - m_new)
        l = alpha * l_ref[...] + p_self
        acc = alpha * acc_ref[...] + rnd(p_self) * rnd(vs_ref[0])
        o_ref[0] = jnp.sum(jnp.where(own, acc / l, 0.0), axis=0, keepdims=True).astype(o_ref.dtype)


def _sample_attention(page_table_flat, q_s, bias_past, bias_self, k_s, v_s, cache_k_t, cache_v_t, layer, *,
                      n_pages, npg):
    nb = q_s.shape[0]
    ngrp = n_pages // npg
    kv_specs = _page_specs((None, None, N_HEADS, HEAD_DIM, PAGE_SIZE), layer, n_pages, npg)
    per_seq = lambda n: pl.BlockSpec((1, 1, n), lambda b, g, pt: (b, 0, 0))
    return pl.pallas_call(
        functools.partial(_sample_attn_kernel, npg=npg),
        out_shape=jax.ShapeDtypeStruct((nb, 1, D_ATTN), BF16),
        grid_spec=pltpu.PrefetchScalarGridSpec(
            num_scalar_prefetch=1,
            grid=(nb, ngrp),
            in_specs=[per_seq(D_ATTN),
                      pl.BlockSpec((1, 1, npg * PAGE_SIZE), lambda b, g, pt: (b, 0, g)),
                      per_seq(1), per_seq(D_ATTN), per_seq(D_ATTN)] + kv_specs + kv_specs,
            out_specs=per_seq(D_ATTN),
            scratch_shapes=[pltpu.VMEM((N_HEADS, 1), F32), pltpu.VMEM((N_HEADS, 1), F32),
                            pltpu.VMEM((N_HEADS, D_ATTN), F32)],
        ),
        compiler_params=_cparams(("parallel", "arbitrary")),
        name="sparse_attention_sample",
    )(page_table_flat, q_s, bias_past, bias_self, k_s, v_s, *([cache_k_t] * npg), *([cache_v_t] * npg))


def _pack_layer(l, w_in, b_in, conv_w, conv_b, rg_wa, rg_ba, rg_wi, rg_bi, rg_lam, w_oa, w_ob, w_out,
                ln1_g, ln1_b, w_ff1, w_ff2, ln2_g, ln2_b, w_pe, w_pg, b_pg):
    d_rnn = conv_w.shape[-1]
    w, b = w_in[l], b_in[l]
    n_idx = 4 * D_ATTN
    ki0, wi0, r0 = n_idx, n_idx + IDX_DIM, n_idx + IDX_DIM + IDX_HEADS

    def pad_cols(a, n):
        return jnp.pad(a, ((0, 0), (0, n - a.shape[1])))

    b2 = b[None, :]
    w_head = jnp.concatenate([w[:, :n_idx], pad_cols(w[:, ki0:wi0], LANES), pad_cols(w[:, wi0:r0], LANES)], axis=1)
    b_head = jnp.concatenate([b2[:, :n_idx], pad_cols(b2[:, ki0:wi0], LANES), pad_cols(b2[:, wi0:r0], LANES)], axis=1)
    vec = lambda a: a[l][None, :]
    return dict(
        w_head=w_head.astype(BF16), b_head=b_head, w_rest=w[:, r0:].astype(BF16), b_rest=b2[:, r0:],
        conv_w=conv_w[l], conv_b=vec(conv_b), rg_wa=rg_wa[l].astype(BF16), rg_ba=vec(rg_ba),
        rg_wi=rg_wi[l].astype(BF16), rg_bi=vec(rg_bi), rg_lam=vec(rg_lam),
        w_oa=w_oa[l].astype(BF16), w_ob=w_ob[l].astype(BF16), w_out=w_out[l].astype(BF16),
        ln1_g=vec(ln1_g), ln1_b=vec(ln1_b), w_ff1=w_ff1[l].astype(BF16), w_ff2=w_ff2[l].astype(BF16),
        ln2_g=vec(ln2_g), ln2_b=vec(ln2_b), w_pe=w_pe[l].astype(BF16), w_pg=w_pg[l].astype(BF16), b_pg=vec(b_pg),
        d_rnn=d_rnn)


def kernel(x_prompt, x_sample, p_prompt, p_sample, cache_k, cache_v, cache_kidx, state_h, state_conv, page_table, w_in, b_in, conv_w, conv_b, rg_wa, rg_ba, rg_wi, rg_bi, rg_lam, w_oa, w_ob, w_out, ln1_g, ln1_b, w_ff1, w_ff2, ln2_g, ln2_b, w_pe, w_pg, b_pg):
    batch, seq, d_model = x_prompt.shape
    nb, dec_seq, _ = x_sample.shape
    assert dec_seq == 1
    depth = w_in.shape[0]
    d_rnn = conv_w.shape[-1]
    n_pages = page_table.shape[1]
    past = n_pages * PAGE_SIZE
    alpha = (2.0 * depth) ** 0.25
    k_prompt_sel = min(TOPK_MAX, seq // 4)
    k_sample_sel = min(TOPK_MAX, (past + dec_seq) // 4)
    m = batch * seq
    tm = min(256, m)
    npg = math.gcd(n_pages, 16)
    npg_idx = math.gcd(n_pages, 32)
    pt_flat = page_table.reshape(-1).astype(I32)
    cache_k_t = jnp.transpose(cache_k, (0, 1, 3, 4, 2))
    cache_v_t = jnp.transpose(cache_v, (0, 1, 3, 4, 2))
    cache_kidx_t = jnp.transpose(cache_kidx, (0, 1, 3, 2))

    xp = x_prompt.reshape(m, d_model)
    xs = x_sample.reshape(nb, d_model)
    outs = [[] for _ in range(10)]
    for l in range(depth):
        lw = _pack_layer(l, w_in, b_in, conv_w, conv_b, rg_wa, rg_ba, rg_wi, rg_bi, rg_lam, w_oa, w_ob, w_out,
                         ln1_g, ln1_b, w_ff1, w_ff2, ln2_g, ln2_b, w_pe, w_pg, b_pg)
        rg = (lw["conv_w"], lw["conv_b"], lw["rg_wa"], lw["rg_ba"], lw["rg_wi"], lw["rg_bi"], lw["rg_lam"])

        (qt4, kx, vt, k32, v32, qpt, kp, ki32, wit, xr, gr, ga, gb) = _inproj(
            xp, lw["w_head"], lw["b_head"], lw["w_rest"], lw["b_rest"], prompt=True, tm=tm)
        bias = _index_topk(qpt, wit, kp, batch=batch, seq=seq, k=k_prompt_sel)
        o_attn = _attention(qt4, kx, vt, bias, batch=batch, seq=seq)
        y_rnn, h_last = _rglru(xr, gr, *rg, batch=batch, seq=seq)
        xp = _post(xp, o_attn, y_rnn, ga, gb, p_prompt[l].reshape(m, -1), lw, alpha=alpha, tm=tm,
                   name="post_prompt")
        outs[0].append(k32.reshape(batch, seq, N_HEADS, HEAD_DIM))
        outs[1].append(v32.reshape(batch, seq, N_HEADS, HEAD_DIM))
        outs[2].append(ki32.reshape(batch, seq, IDX_DIM))
        outs[3].append(h_last.reshape(batch, d_rnn))
        outs[4].append(xr.reshape(batch, seq, d_rnn)[:, seq - (CONV_W - 1):])

        (q_s, k_s, v_s, qp_s, ki_s, wi_s, ss_s, xr_s, gr_s, ga_s, gb_s) = _inproj(
            xs, lw["w_head"], lw["b_head"], lw["w_rest"], lw["b_rest"], prompt=False, tm=nb)
        scores_past = _sample_scores(pt_flat, jnp.swapaxes(qp_s, 0, 1), wi_s.reshape(nb, IDX_HEADS, 1),
                                     cache_kidx_t, l, n_pages=n_pages, npg=npg_idx)
        bias_s = _sample_select(scores_past.reshape(nb, past), ss_s, k=k_sample_sel)
        o_s = _sample_attention(
            pt_flat, q_s.reshape(nb, 1, D_ATTN), bias_s[:, :past].reshape(nb, 1, past),
            bias_s[:, past:past + 1].reshape(nb, 1, 1), k_s.reshape(nb, 1, D_ATTN), v_s.reshape(nb, 1, D_ATTN),
            cache_k_t, cache_v_t, l, n_pages=n_pages, npg=npg)
        y_s, h_s = _rglru_step(xr_s, gr_s, jnp.swapaxes(state_conv[l], 0, 1), state_h[l], *rg)
        xs = _post(xs, o_s.reshape(nb, D_ATTN), y_s, ga_s, gb_s, p_sample[l].reshape(nb, -1), lw, alpha=alpha,
                   tm=nb, name="post_sample")
        outs[5].append(k_s.reshape(nb, 1, N_HEADS, HEAD_DIM))
        outs[6].append(v_s.reshape(nb, 1, N_HEADS, HEAD_DIM))
        outs[7].append(ki_s.reshape(nb, 1, IDX_DIM))
        outs[8].append(h_s)
        outs[9].append(jnp.concatenate([state_conv[l][:, 1:], xr_s[:, None, :]], axis=1))

    return (xp.reshape(batch, seq, d_model), xs.reshape(nb, 1, d_model)) + tuple(jnp.stack(o) for o in outs)
```
